```python
import jax, jax.numpy as jnp
from jax import lax
import numpy as np

D_MODEL = 1024
BATCH = 2
SEQ = 8192
DEPTH = 4
DEC_BATCH = 128
DEC_SEQ = 4
PAST_LEN = 8192
PAGE_SIZE = 128

N_HEADS = 16
N_KV_HEADS = 4
HEAD_DIM = D_MODEL // N_HEADS
GROUP = N_HEADS // N_KV_HEADS
QKV_DIM = (N_HEADS + 2 * N_KV_HEADS) * HEAD_DIM
D_FF = 4 * D_MODEL
WINDOW = 128
MOBA_BLOCK = 256
MOBA_TOPK = 3
MOBA_Q_CHUNK = 64
N_MIXERS = 2
N_SWA = (DEPTH + 1) // 2
N_MOBA = DEPTH // 2
RMS_EPS = 1e-6

kernel_name = "swa_sink_moba_hybrid_decode_step"


def alibi_slopes():
    h = jnp.arange(1, N_HEADS + 1, dtype=jnp.float32)
    return jnp.exp2(-8.0 * h / N_HEADS).reshape(N_KV_HEADS, GROUP)


def rms_norm(x, g):
    x32 = x.astype(jnp.float32)
    y = x32 * lax.rsqrt(jnp.mean(x32 * x32, axis=-1, keepdims=True) + RMS_EPS)
    return y.astype(x.dtype) * g


def project_qkv(h, w_qkv, g_q, g_k):
    n, t, _ = h.shape
    qkv = h @ w_qkv
    q, k, v = jnp.split(qkv, [N_HEADS * HEAD_DIM, (N_HEADS + N_KV_HEADS) * HEAD_DIM], axis=-1)
    q = rms_norm(q.reshape(n, t, N_KV_HEADS, GROUP, HEAD_DIM), g_q) * (HEAD_DIM ** -0.5)
    k = rms_norm(k.reshape(n, t, N_KV_HEADS, HEAD_DIM), g_k)
    v = v.reshape(n, t, N_KV_HEADS, HEAD_DIM)
    return q, k, v


def sq_relu_mlp(h, w_up, w_down):
    return jnp.square(jax.nn.relu(h @ w_up)) @ w_down


def swa_core(q, k, v, t_pos, s_pos, sinks):
    s = jnp.einsum('nbqkgd,nblkd->nbkgql', q, k, preferred_element_type=jnp.float32)
    delta = t_pos[:, :, None] - s_pos[:, None, :]
    valid = (delta >= 0) & (delta < WINDOW) & (s_pos[:, None, :] >= 0)
    pen = alibi_slopes()[None, :, :, None, None] * delta[:, None, None].astype(jnp.float32)
    s = jnp.where(valid[:, None, None], s - pen, -jnp.inf)
    sink = sinks.astype(jnp.float32)[:, :, None, None]
    m = jnp.maximum(jnp.max(s, axis=-1, keepdims=True), sink)
    p = jnp.exp(s - m)
    denom = jnp.sum(p, axis=-1, keepdims=True) + jnp.exp(sink - m)
    o = jnp.einsum('nbkgql,nblkd->nbqkgd', (p / denom).astype(v.dtype), v,
                   preferred_element_type=jnp.float32)
    return o.astype(v.dtype)


def swa_prompt(q, k, v, sinks):
    n, t = q.shape[:2]
    nb = t // WINDOW
    qb = q.reshape(n, nb, WINDOW, N_KV_HEADS, GROUP, HEAD_DIM)

    def band(x):
        xp = jnp.concatenate([jnp.zeros_like(x[:, :WINDOW]), x], axis=1)
        xp = xp.reshape(n, nb + 1, WINDOW, N_KV_HEADS, HEAD_DIM)
        return jnp.concatenate([xp[:, :-1], xp[:, 1:]], axis=2)

    start = jnp.arange(nb)[:, None] * WINDOW
    t_pos = start + jnp.arange(WINDOW)[None]
    s_pos = start - WINDOW + jnp.arange(2 * WINDOW)[None]
    o = swa_core(qb, band(k), band(v), t_pos, s_pos, sinks)
    return o.reshape(n, t, N_HEADS * HEAD_DIM)


def swa_sample(q, k, v, win_k, win_v, past_len, sinks):
    n, t = q.shape[:2]
    wc = win_k.shape[1]
    kk = jnp.concatenate([win_k, k], axis=1)
    vv = jnp.concatenate([win_v, v], axis=1)
    t_pos = (past_len + jnp.arange(t))[None]
    s_pos = (past_len - wc + jnp.arange(wc + t))[None]
    o = swa_core(q[:, None], kk[:, None], vv[:, None], t_pos, s_pos, sinks)
    return o.reshape(n, t, N_HEADS * HEAD_DIM), kk[:, -wc:], vv[:, -wc:]


def moba_select(q, means, t_pos, k_sel):
    g = jnp.einsum('nqkgd,njkd->nqkgj', q, means, preferred_element_type=jnp.float32)
    n_past = t_pos // MOBA_BLOCK
    cand = jnp.arange(means.shape[1])[None, :] < n_past[:, None]
    g = jnp.where(cand[:, None, None, :], g, -jnp.inf)
    _, idx = lax.top_k(g, k_sel)
    valid = idx < n_past[:, None, None, None]
    return idx, valid


def moba_core(q, t_pos, own_k, own_v, own_pos, sel=None):
    slopes = alibi_slopes()
    own_start = (t_pos // MOBA_BLOCK) * MOBA_BLOCK
    so = jnp.einsum('nqkgd,nlkd->nqkgl', q, own_k, preferred_element_type=jnp.float32)
    d_o = t_pos[:, None] - own_pos[None, :]
    valid_o = (d_o >= 0) & (own_pos[None, :] >= own_start[:, None])
    so = jnp.where(valid_o[:, None, None, :],
                   so - slopes[..., None] * d_o[:, None, None, :].astype(jnp.float32), -jnp.inf)
    m = jnp.max(so, axis=-1)
    if sel is not None:
        ks, vs, idx, valid_s = sel
        ss = jnp.einsum('nqkgd,nqkgjld->nqkgjl', q, ks, preferred_element_type=jnp.float32)
        s_pos = idx[..., None] * MOBA_BLOCK + jnp.arange(MOBA_BLOCK)
        d_s = (t_pos[None, :, None, None, None, None] - s_pos).astype(jnp.float32)
        ss = jnp.where(valid_s[..., None], ss - slopes[:, :, None, None] * d_s, -jnp.inf)
        m = jnp.maximum(m, jnp.max(ss, axis=(-2, -1)))
    po = jnp.exp(so - m[..., None])
    denom = jnp.sum(po, axis=-1)
    o = jnp.einsum('nqkgl,nlkd->nqkgd', po.astype(own_v.dtype), own_v,
                   preferred_element_type=jnp.float32)
    if sel is not None:
        ps = jnp.exp(ss - m[..., None, None])
        denom = denom + jnp.sum(ps, axis=(-2, -1))
        o = o + jnp.einsum('nqkgjl,nqkgjld->nqkgd', ps.astype(vs.dtype), vs,
                           preferred_element_type=jnp.float32)
    return (o / denom[..., None]).astype(own_v.dtype)


def moba_prompt(q, k, v):
    n, t = q.shape[:2]
    nb = -(-t // MOBA_BLOCK)
    pad = nb * MOBA_BLOCK - t
    kp = jnp.pad(k, ((0, 0), (0, pad), (0, 0), (0, 0)))
    vp = jnp.pad(v, ((0, 0), (0, pad), (0, 0), (0, 0)))
    kb = kp.reshape(n, nb, MOBA_BLOCK, N_KV_HEADS, HEAD_DIM)
    vb = vp.reshape(n, nb, MOBA_BLOCK, N_KV_HEADS, HEAD_DIM)
    means = jnp.mean(kb.astype(jnp.float32), axis=2).astype(k.dtype)
    kbt = kb.transpose(0, 3, 1, 2, 4)
    vbt = vb.transpose(0, 3, 1, 2, 4)
    k_sel = min(MOBA_TOPK, nb - 1)
    n_ix = jnp.arange(n)[:, None, None, None, None]
    kv_ix = jnp.arange(N_KV_HEADS)[None, None, :, None, None]

    def chunk(c):
        start = c * MOBA_Q_CHUNK
        t_pos = start + jnp.arange(MOBA_Q_CHUNK)
        qc = lax.dynamic_slice_in_dim(q, start, MOBA_Q_CHUNK, axis=1)
        own0 = (start // MOBA_BLOCK) * MOBA_BLOCK
        ko = lax.dynamic_slice_in_dim(kp, own0, MOBA_BLOCK, axis=1)
        vo = lax.dynamic_slice_in_dim(vp, own0, MOBA_BLOCK, axis=1)
        own_pos = own0 + jnp.arange(MOBA_BLOCK)
        sel = None
        if k_sel > 0:
            idx, valid = moba_select(qc, means, t_pos, k_sel)
            sel = (kbt[n_ix, kv_ix, idx], vbt[n_ix, kv_ix, idx], idx, valid)
        return moba_core(qc, t_pos, ko, vo, own_pos, sel)

    o = lax.map(chunk, jnp.arange(t // MOBA_Q_CHUNK))
    return o.transpose(1, 0, 2, 3, 4, 5).reshape(n, t, N_HEADS * HEAD_DIM)


def moba_sample(q, k, v, pool_k, pool_v, layer, page_table):
    n, t = q.shape[:2]
    n_pages = page_table.shape[1]
    past_len = n_pages * PAGE_SIZE
    ppb = MOBA_BLOCK // PAGE_SIZE
    n_own = min(ppb, n_pages)
    own_pages = page_table[:, n_pages - n_own:]
    ko = jnp.concatenate([pool_k[layer, own_pages].reshape(n, n_own * PAGE_SIZE, N_KV_HEADS, HEAD_DIM), k], axis=1)
    vo = jnp.concatenate([pool_v[layer, own_pages].reshape(n, n_own * PAGE_SIZE, N_KV_HEADS, HEAD_DIM), v], axis=1)
    own_pos = past_len - n_own * PAGE_SIZE + jnp.arange(n_own * PAGE_SIZE + t)
    n_blk = past_len // MOBA_BLOCK
    k_sel = min(MOBA_TOPK, n_blk)
    means = None
    if k_sel > 0:
        rows = pool_k[layer, page_table[:, :n_blk * ppb]]
        page_sum = jnp.sum(rows, axis=2, dtype=jnp.float32)
        means = (page_sum.reshape(n, n_blk, ppb, N_KV_HEADS, HEAD_DIM).sum(axis=2) / MOBA_BLOCK).astype(k.dtype)
    n_ix = jnp.arange(n)[:, None, None, None, None, None]
    kv_ix = jnp.arange(N_KV_HEADS)[None, None, :, None, None, None]
    page_off = jnp.arange(ppb)

    def token(j):
        qj = lax.dynamic_slice_in_dim(q, j, 1, axis=1)
        tj = (past_len + j)[None]
        sel = None
        if k_sel > 0:
            idx, valid = moba_select(qj, means, tj, k_sel)
            phys = page_table[n_ix, idx[..., None] * ppb + page_off]
            shape = idx.shape + (MOBA_BLOCK, HEAD_DIM)
            ks = pool_k[layer, phys, :, kv_ix].reshape(shape)
            vs = pool_v[layer, phys, :, kv_ix].reshape(shape)
            sel = (ks, vs, idx, valid)
        return moba_core(qj, tj, ko, vo, own_pos, sel)

    o = lax.map(token, jnp.arange(t))
    return o.transpose(1, 0, 2, 3, 4, 5).reshape(n, t, N_HEADS * HEAD_DIM)


def setup_inputs(seed: int = 0) -> dict:
    key = jax.random.key(seed)
    ks = jax.random.split(key, 20)
    f32 = jnp.float32
    n_pages = PAST_LEN // PAGE_SIZE
    n_used = DEC_BATCH * n_pages
    n_pool = n_used + max(1, n_used // 4)
    win_c = min(WINDOW, PAST_LEN)
    page_table = jax.random.permutation(ks[0], n_pool)[:n_used].reshape(DEC_BATCH, n_pages).astype(jnp.int32)
    nrm = lambda k_, shape: jax.random.normal(k_, shape, f32)
    return {
        "x_prompt": nrm(ks[1], (BATCH, SEQ, D_MODEL)),
        "x_sample": nrm(ks[2], (DEC_BATCH, DEC_SEQ, D_MODEL)),
        "cache_swa_k": nrm(ks[3], (N_SWA, DEC_BATCH, win_c, N_KV_HEADS, HEAD_DIM)),
        "cache_swa_v": nrm(ks[4], (N_SWA, DEC_BATCH, win_c, N_KV_HEADS, HEAD_DIM)),
        "cache_moba_k": nrm(ks[5], (N_MOBA, n_pool, PAGE_SIZE, N_KV_HEADS, HEAD_DIM)),
        "cache_moba_v": nrm(ks[6], (N_MOBA, n_pool, PAGE_SIZE, N_KV_HEADS, HEAD_DIM)),
        "page_table": page_table,
        "g_attn": 1.0 + 0.02 * nrm(ks[7], (DEPTH, D_MODEL)),
        "w_qkv": nrm(ks[8], (DEPTH, D_MODEL, QKV_DIM)) * D_MODEL ** -0.5,
        "g_q": 1.0 + 0.02 * nrm(ks[9], (DEPTH, HEAD_DIM)),
        "g_k": 1.0 + 0.02 * nrm(ks[10], (DEPTH, HEAD_DIM)),
        "sinks": 0.5 * nrm(ks[11], (N_SWA, N_HEADS)),
        "w_o": nrm(ks[12], (DEPTH, N_HEADS * HEAD_DIM, D_MODEL)) * (N_HEADS * HEAD_DIM) ** -0.5,
        "g_mlp": 1.0 + 0.02 * nrm(ks[13], (DEPTH, D_MODEL)),
        "w_up": nrm(ks[14], (DEPTH, D_MODEL, D_FF)) * D_MODEL ** -0.5,
        "w_down": nrm(ks[15], (DEPTH, D_FF, D_MODEL)) * D_FF ** -0.5,
    }


def reference(x_prompt, x_sample, cache_swa_k, cache_swa_v, cache_moba_k, cache_moba_v, page_table,
              g_attn, w_qkv, g_q, g_k, sinks, w_o, g_mlp, w_up, w_down):
    past_len = page_table.shape[1] * PAGE_SIZE
    xp, xs = x_prompt, x_sample
    swa_kp, swa_vp, swa_ks, swa_vs = [], [], [], []
    moba_kp, moba_vp, moba_ks, moba_vs = [], [], [], []
    for i in range(DEPTH):
        qp, kp, vp = project_qkv(rms_norm(xp, g_attn[i]), w_qkv[i], g_q[i], g_k[i])
        qs, ks, vs = project_qkv(rms_norm(xs, g_attn[i]), w_qkv[i], g_q[i], g_k[i])
        j = i // N_MIXERS
        if i % N_MIXERS == 0:
            sink = sinks[j].reshape(N_KV_HEADS, GROUP)
            op = swa_prompt(qp, kp, vp, sink)
            osm, wk, wv = swa_sample(qs, ks, vs, cache_swa_k[j], cache_swa_v[j], past_len, sink)
            w = min(WINDOW, kp.shape[1])
            swa_kp.append(kp[:, -w:])
            swa_vp.append(vp[:, -w:])
            swa_ks.append(wk)
            swa_vs.append(wv)
        else:
            op = moba_prompt(qp, kp, vp)
            osm = moba_sample(qs, ks, vs, cache_moba_k, cache_moba_v, j, page_table)
            moba_kp.append(kp)
            moba_vp.append(vp)
            moba_ks.append(ks)
            moba_vs.append(vs)
        xp = xp + op @ w_o[i]
        xs = xs + osm @ w_o[i]
        xp = xp + sq_relu_mlp(rms_norm(xp, g_mlp[i]), w_up[i], w_down[i])
        xs = xs + sq_relu_mlp(rms_norm(xs, g_mlp[i]), w_up[i], w_down[i])
    return (xp, xs,
            jnp.stack(swa_kp), jnp.stack(swa_vp), jnp.stack(swa_ks), jnp.stack(swa_vs),
            jnp.stack(moba_kp), jnp.stack(moba_vp), jnp.stack(moba_ks), jnp.stack(moba_vs))
```

```python
import functools

import jax
import jax.numpy as jnp
from jax import lax
from jax.experimental import pallas as pl
from jax.experimental.pallas import tpu as pltpu

F32 = jnp.float32
BF16 = jnp.bfloat16

D_MODEL = 1024
N_HEADS = 16
N_KV = 4
GROUP = N_HEADS // N_KV
HD = D_MODEL // N_HEADS
QD = N_HEADS * HD
KVD = N_KV * HD
QKV_DIM = QD + 2 * KVD
D_FF = 4 * D_MODEL
WINDOW = 128
BLK = 256
TOPK = 3
PAGE = 128
EPS = 1e-6
NEG = -1e30

ROW_TILE = 512
FF_CHUNK = 1024
SWA_SEQS_PER_STEP = 8
MOBA_PAGES_PER_STEP = 8
VMEM_LIMIT = 56 * 1024 * 1024


def _params(*sem):
    return pltpu.CompilerParams(dimension_semantics=sem, vmem_limit_bytes=VMEM_LIMIT)


def _const_spec(shape):
    nd = len(shape)
    return pl.BlockSpec(shape, lambda *_: (0,) * nd, pipeline_mode=pl.Buffered(1))


def _split_bf16(x):
    hi = x.astype(BF16)
    lo = (x - hi.astype(F32)).astype(BF16)
    return hi, lo


def _qkv_kernel(x_ref, g_ref, w_ref, gq_ref, gk_ref, q_ref, k_ref, v_ref, *mean_refs):
    x = x_ref[...]
    ms = jnp.mean(x * x, axis=-1, keepdims=True)
    xn = (x * lax.rsqrt(ms + EPS)) * g_ref[...]
    qkv = jnp.dot(xn.astype(BF16), w_ref[...], preferred_element_type=F32)

    r = lax.broadcasted_iota(jnp.int32, (KVD, KVD), 0) // HD
    c = lax.broadcasted_iota(jnp.int32, (KVD, KVD), 1) // HD
    seg = jnp.where(r == c, 1.0, 0.0).astype(BF16)

    def head_norm(z, gain):
        hi, lo = _split_bf16(z * z)
        tot = (jnp.dot(hi, seg, preferred_element_type=F32)
               + jnp.dot(lo, seg, preferred_element_type=F32))
        return (z * lax.rsqrt(tot * (1.0 / HD) + EPS)) * gain

    for cidx in range(QD // KVD):
        z = qkv[:, cidx * KVD:(cidx + 1) * KVD]
        q_ref[:, cidx * KVD:(cidx + 1) * KVD] = (
            head_norm(z, gq_ref[...]) * (HD ** -0.5)).astype(q_ref.dtype)
    k = head_norm(qkv[:, QD:QD + KVD], gk_ref[...])
    k_ref[...] = k
    v_ref[...] = qkv[:, QD + KVD:]
    if mean_refs:
        (mean_ref,) = mean_refs
        for b in range(ROW_TILE // BLK):
            mean_ref[0, b:b + 1, :] = jnp.sum(
                k[b * BLK:(b + 1) * BLK], axis=0, keepdims=True) * (1.0 / BLK)


def _qkv_call(x, g_attn, wqkv, gq, gk, with_means):
    rows = x.shape[0]
    assert rows % ROW_TILE == 0
    steps = rows // ROW_TILE
    out_shape = [jax.ShapeDtypeStruct((rows, QD), BF16),
                 jax.ShapeDtypeStruct((rows, KVD), F32),
                 jax.ShapeDtypeStruct((rows, KVD), F32)]
    out_specs = [pl.BlockSpec((ROW_TILE, QD), lambda i: (i, 0)),
                 pl.BlockSpec((ROW_TILE, KVD), lambda i: (i, 0)),
                 pl.BlockSpec((ROW_TILE, KVD), lambda i: (i, 0))]
    if with_means:
        out_shape.append(jax.ShapeDtypeStruct((steps, ROW_TILE // BLK, KVD), F32))
        out_specs.append(pl.BlockSpec((1, ROW_TILE // BLK, KVD), lambda i: (i, 0, 0)))
    return pl.pallas_call(
        _qkv_kernel,
        grid=(steps,),
        in_specs=[pl.BlockSpec((ROW_TILE, D_MODEL), lambda i: (i, 0)),
                  _const_spec((1, D_MODEL)),
                  _const_spec((D_MODEL, QKV_DIM)),
                  _const_spec((1, KVD)),
                  _const_spec((1, KVD))],
        out_specs=out_specs,
        out_shape=out_shape,
        compiler_params=_params("arbitrary"),
        name="qkv_proj",
    )(x, g_attn, wqkv, gq, gk)


def _mlp_kernel(x_ref, a_ref, wo_ref, g_ref, wup_ref, wdn_ref, o_ref):
    x1 = x_ref[...] + jnp.dot(a_ref[...], wo_ref[...], preferred_element_type=F32)
    ms = jnp.mean(x1 * x1, axis=-1, keepdims=True)
    h = ((x1 * lax.rsqrt(ms + EPS)) * g_ref[...]).astype(BF16)
    acc = x1
    for c in range(D_FF // FF_CHUNK):
        u = jnp.dot(h, wup_ref[:, c * FF_CHUNK:(c + 1) * FF_CHUNK], preferred_element_type=F32)
        u = jnp.square(jnp.maximum(u, 0.0)).astype(BF16)
        acc = acc + jnp.dot(u, wdn_ref[c * FF_CHUNK:(c + 1) * FF_CHUNK, :],
                            preferred_element_type=F32)
    o_ref[...] = acc


def _mlp_call(x, attn, wo, g_mlp, wup, wdn):
    rows = x.shape[0]
    assert rows % ROW_TILE == 0
    return pl.pallas_call(
        _mlp_kernel,
        grid=(rows // ROW_TILE,),
        in_specs=[pl.BlockSpec((ROW_TILE, D_MODEL), lambda i: (i, 0)),
                  pl.BlockSpec((ROW_TILE, QD), lambda i: (i, 0)),
                  _const_spec((QD, D_MODEL)),
                  _const_spec((1, D_MODEL)),
                  _const_spec((D_MODEL, D_FF)),
                  _const_spec((D_FF, D_MODEL))],
        out_specs=pl.BlockSpec((ROW_TILE, D_MODEL), lambda i: (i, 0)),
        out_shape=jax.ShapeDtypeStruct((rows, D_MODEL), F32),
        compiler_params=_params("arbitrary"),
        name="oproj_mlp",
    )(x, attn, wo, g_mlp, wup, wdn)


def _group_queries(qT_ref, kv):
    qs = jnp.concatenate(
        [qT_ref[0, (kv * GROUP + g) * HD:(kv * GROUP + g + 1) * HD, :] for g in range(GROUP)],
        axis=1)
    zero = jnp.zeros_like(qs)
    return jnp.concatenate([zero] * kv + [qs] + [zero] * (N_KV - 1 - kv), axis=0)


def _tile_offsets():
    lanes = GROUP * BLK
    rr = (lax.broadcasted_iota(jnp.int32, (1, lanes), 1) % BLK).astype(F32)
    cc = lax.broadcasted_iota(jnp.int32, (BLK, 1), 0).astype(F32)
    return rr - cc


def _store_heads(o_ref, kv, o):
    for g in range(GROUP):
        h = kv * GROUP + g
        o_ref[0, h * HD:(h + 1) * HD, :] = o[:, g * BLK:(g + 1) * BLK].astype(o_ref.dtype)


def _swa_prompt_kernel(qT_ref, kc_ref, kp_ref, vc_ref, vp_ref, slope_ref, sink_ref, o_ref):
    b = pl.program_id(1)
    dcur = _tile_offsets()
    dprev = dcur + float(BLK)
    prev_bias = jnp.where(b > 0, 0.0, NEG)
    for kv in range(N_KV):
        qs = _group_queries(qT_ref, kv)
        slope = slope_ref[kv]
        sink = sink_ref[kv]
        s1 = jnp.dot(kc_ref[0, 0], qs, preferred_element_type=F32)
        s1 = jnp.where((dcur >= 0.0) & (dcur < float(WINDOW)), s1 - slope * dcur, NEG)
        s2 = jnp.dot(kp_ref[0, 0], qs, preferred_element_type=F32)
        s2 = jnp.where(dprev < float(WINDOW), s2 - slope * dprev, NEG) + prev_bias
        m = jnp.maximum(jnp.maximum(jnp.max(s1, axis=0, keepdims=True),
                                    jnp.max(s2, axis=0, keepdims=True)), sink)
        p1 = jnp.exp(s1 - m)
        p2 = jnp.exp(s2 - m)
        denom = (jnp.sum(p1, axis=0, keepdims=True) + jnp.sum(p2, axis=0, keepdims=True)
                 + jnp.exp(sink - m))
        acc = (jnp.dot(vc_ref[0, 0, kv * HD:(kv + 1) * HD, :], p1.astype(BF16),
                       preferred_element_type=F32)
               + jnp.dot(vp_ref[0, 0, kv * HD:(kv + 1) * HD, :], p2.astype(BF16),
                         preferred_element_type=F32))
        _store_heads(o_ref, kv, acc / denom)


def _swa_prompt_call(qT, kb4, vT4, slope_rows, sink_rows):
    n, _, t = qT.shape
    nb = t // BLK
    cur = lambda i, b: (i, b, 0, 0)
    prev = lambda i, b: (i, jnp.maximum(b - 1, 0), 0, 0)
    return pl.pallas_call(
        _swa_prompt_kernel,
        grid=(n, nb),
        in_specs=[pl.BlockSpec((1, QD, BLK), lambda i, b: (i, 0, b)),
                  pl.BlockSpec((1, 1, BLK, KVD), cur),
                  pl.BlockSpec((1, 1, BLK, KVD), prev),
                  pl.BlockSpec((1, 1, KVD, BLK), cur),
                  pl.BlockSpec((1, 1, KVD, BLK), prev),
                  pl.BlockSpec((N_KV, 1, GROUP * BLK), lambda i, b: (0, 0, 0)),
                  pl.BlockSpec((N_KV, 1, GROUP * BLK), lambda i, b: (0, 0, 0))],
        out_specs=pl.BlockSpec((1, QD, BLK), lambda i, b: (i, 0, b)),
        out_shape=jax.ShapeDtypeStruct((n, QD, t), BF16),
        compiler_params=_params("arbitrary", "arbitrary"),
        name="swa_prompt",
    )(qT, kb4, kb4, vT4, vT4, slope_rows, sink_rows)


def _moba_prompt_kernel(qT_ref, k_ref, vT_ref, mean_ref, slope_ref, o_ref,
                        sel_ref, m_ref, l_ref, acc_ref, *, nb):
    b = pl.program_id(1)
    lanes = GROUP * BLK
    dmat = _tile_offsets()
    jj = lax.broadcasted_iota(jnp.int32, (nb, lanes), 0)
    mean_hi, mean_lo = _split_bf16(mean_ref[0])
    for kv in range(N_KV):
        qs = _group_queries(qT_ref, kv)
        slope = slope_ref[kv]
        pen0 = slope * dmat

        gate = (jnp.dot(mean_hi, qs, preferred_element_type=F32)
                + jnp.dot(mean_lo, qs, preferred_element_type=F32))
        gate = jnp.where(jj < b, gate, NEG)
        sel = jnp.zeros((nb, lanes), F32)
        for _ in range(TOPK):
            mx = jnp.max(gate, axis=0, keepdims=True)
            idx = jnp.min(jnp.where(gate == mx, jj, nb), axis=0, keepdims=True)
            hit = jj == jnp.where(mx > 0.5 * NEG, idx, -1)
            sel = jnp.where(hit, 1.0, sel)
            gate = jnp.where(hit, NEG, gate)
        sel_ref[...] = sel

        s = jnp.dot(k_ref[0, b], qs, preferred_element_type=F32)
        s = jnp.where(dmat >= 0.0, s - pen0, NEG)
        m = jnp.max(s, axis=0, keepdims=True)
        p = jnp.exp(s - m)
        m_ref[...] = m
        l_ref[...] = jnp.sum(p, axis=0, keepdims=True)
        acc_ref[...] = jnp.dot(vT_ref[0, b, kv * HD:(kv + 1) * HD, :], p.astype(BF16),
                               preferred_element_type=F32)

        def past_block(j, carry):
            off = slope * ((b - j) * BLK).astype(F32)
            bias = jnp.where(sel_ref[pl.ds(j, 1), :] > 0.5, -off, NEG)
            s = (jnp.dot(k_ref[0, j], qs, preferred_element_type=F32) - pen0) + bias
            m_old = m_ref[...]
            m_new = jnp.maximum(m_old, jnp.max(s, axis=0, keepdims=True))
            alpha = jnp.exp(m_old - m_new)
            p = jnp.exp(s - m_new)
            l_ref[...] = alpha * l_ref[...] + jnp.sum(p, axis=0, keepdims=True)
            acc_ref[...] = alpha * acc_ref[...] + jnp.dot(
                vT_ref[0, j, kv * HD:(kv + 1) * HD, :], p.astype(BF16),
                preferred_element_type=F32)
            m_ref[...] = m_new
            return carry

        lax.fori_loop(0, b, past_block, 0)
        _store_heads(o_ref, kv, acc_ref[...] / l_ref[...])


def _moba_prompt_call(qT, kb4, vT4, means, slope_rows):
    n, _, t = qT.shape
    nb = t // BLK
    lanes = GROUP * BLK
    return pl.pallas_call(
        functools.partial(_moba_prompt_kernel, nb=nb),
        grid=(n, nb),
        in_specs=[pl.BlockSpec((1, QD, BLK), lambda i, b: (i, 0, b)),
                  pl.BlockSpec((1, nb, BLK, KVD), lambda i, b: (i, 0, 0, 0)),
                  pl.BlockSpec((1, nb, KVD, BLK), lambda i, b: (i, 0, 0, 0)),
                  pl.BlockSpec((1, nb, KVD), lambda i, b: (i, 0, 0)),
                  pl.BlockSpec((N_KV, 1, lanes), lambda i, b: (0, 0, 0))],
        out_specs=pl.BlockSpec((1, QD, BLK), lambda i, b: (i, 0, b)),
        out_shape=jax.ShapeDtypeStruct((n, QD, t), BF16),
        scratch_shapes=[pltpu.VMEM((nb, lanes), F32),
                        pltpu.VMEM((1, lanes), F32),
                        pltpu.VMEM((1, lanes), F32),
                        pltpu.VMEM((HD, lanes), F32)],
        compiler_params=_params("arbitrary", "arbitrary"),
        name="moba_prompt",
    )(qT, kb4, vT4, means, slope_rows)


def _new_token_scores(qb, knew, slope, tt, n_new):
    qf = qb.astype(F32)
    out = []
    for t in range(n_new):
        s = jnp.sum(qf * knew[t:t + 1, :], axis=1, keepdims=True)
        d = tt - float(t)
        out.append(jnp.where(d >= 0.0, s - slope * d, NEG))
    return out


def _fold_heads(o):
    rows = o.shape[0]
    r = lax.broadcasted_iota(jnp.int32, (rows, KVD), 0) // (rows // N_KV)
    c = lax.broadcasted_iota(jnp.int32, (rows, KVD), 1) // HD
    o = jnp.where(r == c, o, 0.0)
    step = rows // N_KV
    return o[0:step] + o[step:2 * step] + o[2 * step:3 * step] + o[3 * step:4 * step]


def _swa_sample_kernel(qbd_ref, kc_ref, vc_ref, kn_ref, vn_ref, info_ref, o_ref, *, n_new):
    slope = info_ref[:, 0:1]
    tt = info_ref[:, 1:2]
    sink = info_ref[:, 2:3]
    wc = kc_ref.shape[1]
    cidx = lax.broadcasted_iota(jnp.int32, (1, wc), 1).astype(F32)
    delta = (float(wc) + tt) - cidx
    for s_i in range(qbd_ref.shape[0]):
        qb = qbd_ref[s_i]
        kn = kn_ref[s_i]
        vn = vn_ref[s_i]
        s = lax.dot_general(qb, kc_ref[s_i].astype(BF16), (((1,), (1,)), ((), ())),
                            preferred_element_type=F32)
        s = jnp.where(delta < float(WINDOW), s - slope * delta, NEG)
        s_new = _new_token_scores(qb, kn, slope, tt, n_new)
        m = jnp.maximum(jnp.max(s, axis=1, keepdims=True), sink)
        for sn in s_new:
            m = jnp.maximum(m, sn)
        p = jnp.exp(s - m)
        denom = jnp.sum(p, axis=1, keepdims=True) + jnp.exp(sink - m)
        o = jnp.dot(p.astype(BF16), vc_ref[s_i].astype(BF16), preferred_element_type=F32)
        for t, sn in enumerate(s_new):
            e = jnp.exp(sn - m)
            denom = denom + e
            o = o + e * vn[t:t + 1, :]
        o_ref[s_i] = _fold_heads(o / denom)


def _swa_sample_call(qbd, kc, vc, kn, vn, info):
    nseq, rows, _ = qbd.shape
    wc = kc.shape[1]
    n_new = kn.shape[1]
    sb = SWA_SEQS_PER_STEP
    assert nseq % sb == 0
    return pl.pallas_call(
        functools.partial(_swa_sample_kernel, n_new=n_new),
        grid=(nseq // sb,),
        in_specs=[pl.BlockSpec((sb, rows, KVD), lambda i: (i, 0, 0)),
                  pl.BlockSpec((sb, wc, KVD), lambda i: (i, 0, 0)),
                  pl.BlockSpec((sb, wc, KVD), lambda i: (i, 0, 0)),
                  pl.BlockSpec((sb, n_new, KVD), lambda i: (i, 0, 0)),
                  pl.BlockSpec((sb, n_new, KVD), lambda i: (i, 0, 0)),
                  pl.BlockSpec(info.shape, lambda i: (0, 0))],
        out_specs=pl.BlockSpec((sb, rows // N_KV, KVD), lambda i: (i, 0, 0)),
        out_shape=jax.ShapeDtypeStruct((nseq, rows // N_KV, KVD), F32),
        compiler_params=_params("arbitrary"),
        name="swa_sample",
    )(qbd, kc, vc, kn, vn, info)


def _moba_sample_kernel(pt_ref, *refs, n_new, n_blk, past_len):
    pps = MOBA_PAGES_PER_STEP
    k_refs = refs[:pps]
    v_refs = refs[pps:2 * pps]
    qbd_ref, kn_ref, vn_ref, info_ref, o_ref, m_ref, l_ref, oall_ref, mean_ref = refs[2 * pps:]
    del pt_ref
    step = pl.program_id(1)
    slope = info_ref[:, 0:1]
    tt = info_ref[:, 1:2]
    qb = qbd_ref[0]
    rows = qb.shape[0]
    lane = lax.broadcasted_iota(jnp.int32, (rows, 128), 1)
    cidx = lax.broadcasted_iota(jnp.int32, (1, BLK), 1).astype(F32)
    pen0 = slope * (tt - cidx)

    @pl.when(step == 0)
    def _():
        m_ref[...] = jnp.zeros_like(m_ref)
        l_ref[...] = jnp.zeros_like(l_ref)

    pages_per_blk = BLK // PAGE
    for jb in range(pps // pages_per_blk):
        j = step * (pps // pages_per_blk) + jb
        kp = [k_refs[jb * pages_per_blk + i][0, 0] for i in range(pages_per_blk)]
        vp = [v_refs[jb * pages_per_blk + i][0, 0] for i in range(pages_per_blk)]
        ksum = kp[0].sum(axis=0, keepdims=True)
        for extra in kp[1:]:
            ksum = ksum + extra.sum(axis=0, keepdims=True)
        mean_ref[pl.ds(j, 1), :] = ksum * (1.0 / BLK)
        kblk = jnp.concatenate(kp, axis=0).astype(BF16)
        vblk = jnp.concatenate(vp, axis=0).astype(BF16)
        s = lax.dot_general(qb, kblk, (((1,), (1,)), ((), ())),
                            preferred_element_type=F32) - pen0
        mj = jnp.max(s, axis=1, keepdims=True)
        p = jnp.exp(s - mj)
        m_ref[...] = jnp.where(lane == j, mj, m_ref[...])
        l_ref[...] = jnp.where(lane == j, jnp.sum(p, axis=1, keepdims=True), l_ref[...])
        oall_ref[j] = jnp.dot(p.astype(BF16), vblk, preferred_element_type=F32)

    @pl.when(step == pl.num_programs(1) - 1)
    def _():
        kn = kn_ref[0]
        vn = vn_ref[0]
        mean_hi, mean_lo = _split_bf16(mean_ref[...])
        dn = (((1,), (1,)), ((), ()))
        gate = (lax.dot_general(qb, mean_hi, dn, preferred_element_type=F32)
                + lax.dot_general(qb, mean_lo, dn, preferred_element_type=F32))
        bl = lax.broadcasted_iota(jnp.int32, (rows, n_blk), 1)
        sel = jnp.zeros((rows, n_blk), jnp.bool_)
        for _ in range(min(TOPK, n_blk)):
            mx = jnp.max(gate, axis=1, keepdims=True)
            idx = jnp.min(jnp.where(gate == mx, bl, n_blk), axis=1, keepdims=True)
            hit = bl == idx
            sel = sel | hit
            gate = jnp.where(hit, NEG, gate)
        off = slope * (float(past_len) - bl.astype(F32) * float(BLK))
        mt = jnp.where(sel, m_ref[:, :n_blk] - off, NEG)
        s_new = _new_token_scores(qb, kn, slope, tt, n_new)
        m = jnp.max(mt, axis=1, keepdims=True)
        for sn in s_new:
            m = jnp.maximum(m, sn)
        w = jnp.where(sel, jnp.exp(mt - m), 0.0)
        denom = jnp.sum(w * l_ref[:, :n_blk], axis=1, keepdims=True)
        o = jnp.zeros((rows, KVD), F32)
        for t, sn in enumerate(s_new):
            e = jnp.exp(sn - m)
            denom = denom + e
            o = o + e * vn[t:t + 1, :]
        for j in range(n_blk):
            wj = jnp.sum(jnp.where(bl == j, w, 0.0), axis=1, keepdims=True)
            o = o + wj * oall_ref[j]
        o_ref[0] = _fold_heads(o / denom)


def _moba_sample_call(page_table, pool_k, pool_v, layer, qbd, kn, vn, info):
    nseq, rows, _ = qbd.shape
    n_pages = page_table.shape[1]
    n_new = kn.shape[1]
    pps = MOBA_PAGES_PER_STEP
    assert n_pages % pps == 0 and pps % (BLK // PAGE) == 0
    n_blk = n_pages * PAGE // BLK
    past_len = n_pages * PAGE
    assert past_len % BLK == 0 and n_new <= BLK and n_blk <= 128

    def page_spec(i):
        return pl.BlockSpec((1, 1, PAGE, KVD),
                            lambda s, g, pt: (layer, pt[s, g * pps + i], 0, 0))

    grid_spec = pltpu.PrefetchScalarGridSpec(
        num_scalar_prefetch=1,
        grid=(nseq, n_pages // pps),
        in_specs=([page_spec(i) for i in range(pps)] + [page_spec(i) for i in range(pps)]
                  + [pl.BlockSpec((1, rows, KVD), lambda s, g, pt: (s, 0, 0)),
                     pl.BlockSpec((1, n_new, KVD), lambda s, g, pt: (s, 0, 0)),
                     pl.BlockSpec((1, n_new, KVD), lambda s, g, pt: (s, 0, 0)),
                     pl.BlockSpec(info.shape, lambda s, g, pt: (0, 0))]),
        out_specs=pl.BlockSpec((1, rows // N_KV, KVD), lambda s, g, pt: (s, 0, 0)),
        scratch_shapes=[pltpu.VMEM((rows, 128), F32),
                        pltpu.VMEM((rows, 128), F32),
                        pltpu.VMEM((n_blk, rows, KVD), F32),
                        pltpu.VMEM((n_blk, KVD), F32)])
    return pl.pallas_call(
        functools.partial(_moba_sample_kernel, n_new=n_new, n_blk=n_blk, past_len=past_len),
        grid_spec=grid_spec,
        out_shape=jax.ShapeDtypeStruct((nseq, rows // N_KV, KVD), F32),
        compiler_params=_params("arbitrary", "arbitrary"),
        name="moba_sample",
    )(page_table, *([pool_k] * pps), *([pool_v] * pps), qbd, kn, vn, info)


def _block_diag_queries(q, nseq, n_new):
    qr = q.reshape(nseq, n_new, N_KV, GROUP, HD).transpose(0, 2, 1, 3, 4)
    eye = jnp.eye(N_KV, dtype=q.dtype)
    qbd = qr[:, :, :, :, None, :] * eye[None, :, None, None, :, None]
    return qbd.reshape(nseq, N_KV * n_new * GROUP, KVD)


def _unfold_sample_out(o, nseq, n_new):
    o = o.reshape(nseq, n_new, GROUP, N_KV, HD).transpose(0, 1, 3, 2, 4)
    return o.reshape(nseq * n_new, QD).astype(BF16)


def kernel(x_prompt, x_sample, cache_swa_k, cache_swa_v, cache_moba_k, cache_moba_v, page_table,
           g_attn, w_qkv, g_q, g_k, sinks, w_o, g_mlp, w_up, w_down):
    n, t, _ = x_prompt.shape
    nseq, n_new, _ = x_sample.shape
    depth = w_qkv.shape[0]
    nb = t // BLK
    assert t % BLK == 0
    n_pool = cache_moba_k.shape[1]

    head = jnp.arange(1, N_HEADS + 1, dtype=F32)
    slopes = jnp.exp2(-8.0 * head / N_HEADS).reshape(N_KV, GROUP)

    def prompt_rows(per_head):
        return jnp.repeat(per_head.astype(F32), BLK, axis=1).reshape(N_KV, 1, GROUP * BLK)

    def sample_rows(per_head):
        return jnp.broadcast_to(per_head.astype(F32)[:, None, :],
                                (N_KV, n_new, GROUP)).reshape(-1)

    tt_rows = jnp.broadcast_to(jnp.arange(n_new, dtype=F32)[None, :, None],
                               (N_KV, n_new, GROUP)).reshape(-1)
    slope_prompt = prompt_rows(slopes)

    wqkv_b = w_qkv.astype(BF16)
    wo_b = w_o.astype(BF16)
    wup_b = w_up.astype(BF16)
    wdn_b = w_down.astype(BF16)
    pool_k = cache_moba_k.reshape(cache_moba_k.shape[0], n_pool, PAGE, KVD)
    pool_v = cache_moba_v.reshape(cache_moba_v.shape[0], n_pool, PAGE, KVD)

    xp = x_prompt.reshape(n * t, D_MODEL)
    xs = x_sample.reshape(nseq * n_new, D_MODEL)
    swa_kp, swa_vp, swa_ks, swa_vs = [], [], [], []
    moba_kp, moba_vp, moba_ks, moba_vs = [], [], [], []
    for i in range(depth):
        is_swa = i % 2 == 0
        j = i // 2
        ga = g_attn[i].reshape(1, D_MODEL)
        gq = jnp.tile(g_q[i], N_KV).reshape(1, KVD)
        gk = jnp.tile(g_k[i], N_KV).reshape(1, KVD)
        res_p = _qkv_call(xp, ga, wqkv_b[i], gq, gk, with_means=not is_swa)
        qp, kp, vp = res_p[:3]
        qs, ks, vs = _qkv_call(xs, ga, wqkv_b[i], gq, gk, with_means=False)

        qT = qp.reshape(n, t, QD).transpose(0, 2, 1)
        kb4 = kp.astype(BF16).reshape(n, nb, BLK, KVD)
        vT4 = vp.astype(BF16).reshape(n, nb, BLK, KVD).transpose(0, 1, 3, 2)
        qbd = _block_diag_queries(qs, nseq, n_new)
        kn = ks.reshape(nseq, n_new, KVD)
        vn = vs.reshape(nseq, n_new, KVD)
        kp5 = kp.reshape(n, t, N_KV, HD)
        vp5 = vp.reshape(n, t, N_KV, HD)
        ks5 = ks.reshape(nseq, n_new, N_KV, HD)
        vs5 = vs.reshape(nseq, n_new, N_KV, HD)
        if is_swa:
            sink = sinks[j].reshape(N_KV, GROUP)
            oT = _swa_prompt_call(qT, kb4, vT4, slope_prompt, prompt_rows(sink))
            info = jnp.stack([sample_rows(slopes), tt_rows, sample_rows(sink),
                              jnp.zeros_like(tt_rows)], axis=1)
            wc = cache_swa_k.shape[2]
            o_s = _swa_sample_call(qbd, cache_swa_k[j].reshape(nseq, wc, KVD),
                                   cache_swa_v[j].reshape(nseq, wc, KVD), kn, vn, info)
            w = min(WINDOW, t)
            swa_kp.append(kp5[:, -w:])
            swa_vp.append(vp5[:, -w:])
            swa_ks.append(jnp.concatenate([cache_swa_k[j], ks5], axis=1)[:, -wc:])
            swa_vs.append(jnp.concatenate([cache_swa_v[j], vs5], axis=1)[:, -wc:])
        else:
            means = res_p[3].reshape(n, nb, KVD)
            oT = _moba_prompt_call(qT, kb4, vT4, means, slope_prompt)
            info = jnp.stack([sample_rows(slopes), tt_rows, jnp.zeros_like(tt_rows),
                              jnp.zeros_like(tt_rows)], axis=1)
            o_s = _moba_sample_call(page_table, pool_k, pool_v, j, qbd, kn, vn, info)
            moba_kp.append(kp5)
            moba_vp.append(vp5)
            moba_ks.append(ks5)
            moba_vs.append(vs5)
        attn_p = oT.transpose(0, 2, 1).reshape(n * t, QD)
        attn_s = _unfold_sample_out(o_s, nseq, n_new)
        gm = g_mlp[i].reshape(1, D_MODEL)
        xp = _mlp_call(xp, attn_p, wo_b[i], gm, wup_b[i], wdn_b[i])
        xs = _mlp_call(xs, attn_s, wo_b[i], gm, wup_b[i], wdn_b[i])
    return (xp.reshape(n, t, D_MODEL), xs.reshape(nseq, n_new, D_MODEL),
            jnp.stack(swa_kp), jnp.stack(swa_vp), jnp.stack(swa_ks), jnp.stack(swa_vs),
            jnp.stack(moba_kp), jnp.stack(moba_vp), jnp.stack(moba_ks), jnp.stack(moba_vs))
```

```python
import functools
import math

import jax
import jax.numpy as jnp
from jax import lax
from jax.experimental import pallas as pl
from jax.experimental.pallas import tpu as pltpu

F32 = jnp.float32
BF16 = jnp.bfloat16

D_MODEL = 1024
N_HEADS = 16
N_KV = 4
GROUP = N_HEADS // N_KV
HD = D_MODEL // N_HEADS
QD = N_HEADS * HD
KVD = N_KV * HD
QKV_DIM = QD + 2 * KVD
D_FF = 4 * D_MODEL
WINDOW = 128
BLK = 256
TOPK = 3
PAGE = 128
EPS = 1e-6
NEG = -1e30
LOG2E = math.log2(math.e)

KAUG = 2 * HD
N_SLOPE_PARTS = 3
LANES = GROUP * BLK

ROW_TILE = 512
FF_CHUNK = 1024
SWA_SEQS_PER_STEP = 8
MOBA_PAGES_PER_STEP = 16
VMEM_LIMIT = 56 * 1024 * 1024


def _params(*sem):
    return pltpu.CompilerParams(dimension_semantics=sem, vmem_limit_bytes=VMEM_LIMIT)


def _const_spec(shape):
    nd = len(shape)
    return pl.BlockSpec(shape, lambda *_: (0,) * nd, pipeline_mode=pl.Buffered(1))


def _split_bf16(x):
    hi = x.astype(BF16)
    lo = (x - hi.astype(F32)).astype(BF16)
    return hi, lo


def _qkv_kernel(x_ref, g_ref, w_ref, gq_ref, gk_ref, q_ref, k_ref, v_ref, *mean_refs):
    x = x_ref[...]
    ms = jnp.mean(x * x, axis=-1, keepdims=True)
    xn = (x * lax.rsqrt(ms + EPS)) * g_ref[...]
    qkv = jnp.dot(xn.astype(BF16), w_ref[...], preferred_element_type=F32)

    r = lax.broadcasted_iota(jnp.int32, (KVD, KVD), 0) // HD
    c = lax.broadcasted_iota(jnp.int32, (KVD, KVD), 1) // HD
    seg = jnp.where(r == c, 1.0, 0.0).astype(BF16)

    def head_norm(z, gain):
        hi, lo = _split_bf16(z * z)
        tot = (jnp.dot(hi, seg, preferred_element_type=F32)
               + jnp.dot(lo, seg, preferred_element_type=F32))
        return (z * lax.rsqrt(tot * (1.0 / HD) + EPS)) * gain

    for cidx in range(QD // KVD):
        z = qkv[:, cidx * KVD:(cidx + 1) * KVD]
        q_ref[:, cidx * KVD:(cidx + 1) * KVD] = (
            head_norm(z, gq_ref[...]) * (HD ** -0.5 * LOG2E)).astype(q_ref.dtype)
    k = head_norm(qkv[:, QD:QD + KVD], gk_ref[...])
    k_ref[...] = k
    v_ref[...] = qkv[:, QD + KVD:]
    if mean_refs:
        (mean_ref,) = mean_refs
        for b in range(ROW_TILE // BLK):
            mean_ref[0, b:b + 1, :] = jnp.sum(
                k[b * BLK:(b + 1) * BLK], axis=0, keepdims=True) * (1.0 / BLK)


def _qkv_call(x, g_attn, wqkv, gq, gk, with_means):
    rows = x.shape[0]
    assert rows % ROW_TILE == 0
    steps = rows // ROW_TILE
    out_shape = [jax.ShapeDtypeStruct((rows, QD), BF16),
                 jax.ShapeDtypeStruct((rows, KVD), F32),
                 jax.ShapeDtypeStruct((rows, KVD), F32)]
    out_specs = [pl.BlockSpec((ROW_TILE, QD), lambda i: (i, 0)),
                 pl.BlockSpec((ROW_TILE, KVD), lambda i: (i, 0)),
                 pl.BlockSpec((ROW_TILE, KVD), lambda i: (i, 0))]
    if with_means:
        out_shape.append(jax.ShapeDtypeStruct((steps, ROW_TILE // BLK, KVD), F32))
        out_specs.append(pl.BlockSpec((1, ROW_TILE // BLK, KVD), lambda i: (i, 0, 0)))
    return pl.pallas_call(
        _qkv_kernel,
        grid=(steps,),
        in_specs=[pl.BlockSpec((ROW_TILE, D_MODEL), lambda i: (i, 0)),
                  _const_spec((1, D_MODEL)),
                  _const_spec((D_MODEL, QKV_DIM)),
                  _const_spec((1, KVD)),
                  _const_spec((1, KVD))],
        out_specs=out_specs,
        out_shape=out_shape,
        compiler_params=_params("arbitrary"),
        name="qkv_proj",
    )(x, g_attn, wqkv, gq, gk)


def _mlp_kernel(x_ref, a_ref, wo_ref, g_ref, wup_ref, wdn_ref, o_ref):
    x1 = x_ref[...] + jnp.dot(a_ref[...], wo_ref[...], preferred_element_type=F32)
    ms = jnp.mean(x1 * x1, axis=-1, keepdims=True)
    h = ((x1 * lax.rsqrt(ms + EPS)) * g_ref[...]).astype(BF16)
    acc = x1
    for c in range(D_FF // FF_CHUNK):
        u = jnp.dot(h, wup_ref[:, c * FF_CHUNK:(c + 1) * FF_CHUNK], preferred_element_type=F32)
        u = jnp.square(jnp.maximum(u, 0.0)).astype(BF16)
        acc = acc + jnp.dot(u, wdn_ref[c * FF_CHUNK:(c + 1) * FF_CHUNK, :],
                            preferred_element_type=F32)
    o_ref[...] = acc


def _mlp_call(x, attn, wo, g_mlp, wup, wdn):
    rows = x.shape[0]
    assert rows % ROW_TILE == 0
    return pl.pallas_call(
        _mlp_kernel,
        grid=(rows // ROW_TILE,),
        in_specs=[pl.BlockSpec((ROW_TILE, D_MODEL), lambda i: (i, 0)),
                  pl.BlockSpec((ROW_TILE, QD), lambda i: (i, 0)),
                  _const_spec((QD, D_MODEL)),
                  _const_spec((1, D_MODEL)),
                  _const_spec((D_MODEL, D_FF)),
                  _const_spec((D_FF, D_MODEL))],
        out_specs=pl.BlockSpec((ROW_TILE, D_MODEL), lambda i: (i, 0)),
        out_shape=jax.ShapeDtypeStruct((rows, D_MODEL), F32),
        compiler_params=_params("arbitrary"),
        name="oproj_mlp",
    )(x, attn, wo, g_mlp, wup, wdn)


def _aug_queries(qT_ref, saug_ref, kv):
    qs = jnp.concatenate(
        [qT_ref[0, (kv * GROUP + g) * HD:(kv * GROUP + g + 1) * HD, :] for g in range(GROUP)],
        axis=1)
    return qs, jnp.concatenate([qs, saug_ref[kv]], axis=0)


def _query_minus_key_offset():
    rr = lax.broadcasted_iota(jnp.int32, (1, LANES), 1) % BLK
    cc = lax.broadcasted_iota(jnp.int32, (BLK, 1), 0)
    return rr - cc


def _store_heads(o_ref, kv, o):
    for g in range(GROUP):
        h = kv * GROUP + g
        o_ref[0, h * HD:(h + 1) * HD, :] = o[:, g * BLK:(g + 1) * BLK].astype(o_ref.dtype)


def _swa_prompt_kernel(qT_ref, kc_ref, kp_ref, vc_ref, vp_ref, saug_ref, slope_ref, sink_ref,
                       o_ref):
    b = pl.program_id(1)
    d = _query_minus_key_offset()
    cur_ok = (d >= 0) & (d < WINDOW)
    prev_ok = d < WINDOW - BLK
    prev_bias = jnp.where(b > 0, 0.0, NEG)
    rr = (lax.broadcasted_iota(jnp.int32, (1, LANES), 1) % BLK).astype(F32)
    for kv in range(N_KV):
        _, qaug = _aug_queries(qT_ref, saug_ref, kv)
        slope = slope_ref[kv]
        sink = sink_ref[kv] + slope * rr
        s1 = jnp.dot(kc_ref[0, 0, :, kv * KAUG:(kv + 1) * KAUG], qaug,
                     preferred_element_type=F32)
        s1 = jnp.where(cur_ok, s1, NEG)
        s2 = jnp.dot(kp_ref[0, 0, :, kv * KAUG:(kv + 1) * KAUG], qaug,
                     preferred_element_type=F32)
        s2 = jnp.where(prev_ok, s2, NEG)
        m2 = jnp.max(s2, axis=0, keepdims=True) + (prev_bias - slope * float(BLK))
        m = jnp.maximum(jnp.maximum(jnp.max(s1, axis=0, keepdims=True), m2), sink)
        p1 = jnp.exp2(s1 - m)
        p2 = jnp.exp2(s2 + ((prev_bias - slope * float(BLK)) - m))
        denom = (jnp.sum(p1, axis=0, keepdims=True) + jnp.sum(p2, axis=0, keepdims=True)
                 + jnp.exp2(sink - m))
        acc = (jnp.dot(vc_ref[0, 0, kv * HD:(kv + 1) * HD, :], p1.astype(BF16),
                       preferred_element_type=F32)
               + jnp.dot(vp_ref[0, 0, kv * HD:(kv + 1) * HD, :], p2.astype(BF16),
                         preferred_element_type=F32))
        _store_heads(o_ref, kv, acc / denom)


def _swa_prompt_call(qT, katt, vT4, saug, slope_rows, sink_rows):
    n, _, t = qT.shape
    nb = t // BLK
    cur = lambda i, b: (i, b, 0, 0)
    prev = lambda i, b: (i, jnp.maximum(b - 1, 0), 0, 0)
    whole = lambda i, b: (0, 0, 0)
    return pl.pallas_call(
        _swa_prompt_kernel,
        grid=(n, nb),
        in_specs=[pl.BlockSpec((1, QD, BLK), lambda i, b: (i, 0, b)),
                  pl.BlockSpec((1, 1, BLK, N_KV * KAUG), cur),
                  pl.BlockSpec((1, 1, BLK, N_KV * KAUG), prev),
                  pl.BlockSpec((1, 1, KVD, BLK), cur),
                  pl.BlockSpec((1, 1, KVD, BLK), prev),
                  pl.BlockSpec(saug.shape, whole),
                  pl.BlockSpec(slope_rows.shape, whole),
                  pl.BlockSpec(sink_rows.shape, whole)],
        out_specs=pl.BlockSpec((1, QD, BLK), lambda i, b: (i, 0, b)),
        out_shape=jax.ShapeDtypeStruct((n, QD, t), BF16),
        compiler_params=_params("arbitrary", "arbitrary"),
        name="swa_prompt",
    )(qT, katt, katt, vT4, vT4, saug, slope_rows, sink_rows)


def _moba_prompt_kernel(qT_ref, k_ref, vT_ref, mean_ref, saug_ref, slope_ref, o_ref,
                        qaug_ref, sel_ref, m_ref, l_ref, acc_ref, s_ref, p_ref, *, nb):
    assert N_KV == 4
    b = pl.program_id(1)
    causal = _query_minus_key_offset() >= 0
    jj = lax.broadcasted_iota(jnp.int32, (nb, LANES), 0)
    mean_hi, mean_lo = _split_bf16(mean_ref[0])

    def score_stage(kv, j):
        return jnp.dot(k_ref[0, j, :, kv * KAUG:(kv + 1) * KAUG], qaug_ref[kv],
                       preferred_element_type=F32)

    def value_stage(kv, j, p):
        acc_ref[kv] += jnp.dot(vT_ref[0, j, kv * HD:(kv + 1) * HD, :], p,
                               preferred_element_type=F32)

    def own_softmax_stage(kv, s):
        s = jnp.where(causal, s, NEG)
        m = jnp.max(s, axis=0, keepdims=True)
        p = jnp.exp2(s - m)
        m_ref[kv] = m
        l_ref[kv] = jnp.sum(p, axis=0, keepdims=True)
        acc_ref[kv] = jnp.zeros((HD, LANES), F32)
        return p.astype(BF16)

    def past_softmax_stage(kv, j, dist):
        s = s_ref[kv]
        bias = jnp.where(sel_ref[kv, pl.ds(j, 1), :] > 0.5, -dist * slope_ref[kv], NEG)
        m_old = m_ref[kv]
        m_new = jnp.maximum(m_old, jnp.max(s, axis=0, keepdims=True) + bias)
        p = jnp.exp2(s + (bias - m_new))
        alpha = jnp.exp2(m_old - m_new)
        l_ref[kv] = alpha * l_ref[kv] + jnp.sum(p, axis=0, keepdims=True)
        acc_ref[kv] = alpha * acc_ref[kv]
        m_ref[kv] = m_new
        return p.astype(BF16)

    for kv in range(N_KV):
        qs, qaug = _aug_queries(qT_ref, saug_ref, kv)
        qaug_ref[kv] = qaug

        gate = (jnp.dot(mean_hi[:, kv * HD:(kv + 1) * HD], qs, preferred_element_type=F32)
                + jnp.dot(mean_lo[:, kv * HD:(kv + 1) * HD], qs, preferred_element_type=F32))
        gate = jnp.where(jj < b, gate, NEG)
        sel = jnp.zeros((nb, LANES), F32)
        for _ in range(TOPK):
            mx = jnp.max(gate, axis=0, keepdims=True)
            idx = jnp.min(jnp.where(gate == mx, jj, nb), axis=0, keepdims=True)
            hit = jj == jnp.where(mx > 0.5 * NEG, idx, -1)
            sel = jnp.where(hit, 1.0, sel)
            gate = jnp.where(hit, NEG, gate)
        sel_ref[kv] = sel

    s_own = [score_stage(0, b), score_stage(1, b)]
    p_own = []
    for kv in range(N_KV):
        p_own.append(own_softmax_stage(kv, s_own[kv]))
        if kv + 2 < N_KV:
            s_own.append(score_stage(kv + 2, b))
        else:
            s_ref[kv + 2 - N_KV] = score_stage(kv + 2 - N_KV, 0)
        if kv >= 1:
            value_stage(kv - 1, b, p_own[kv - 1])
    value_stage(N_KV - 1, b, p_own[N_KV - 1])
    p_ref[...] = jnp.zeros_like(p_ref)

    def past_block(j, carry):
        dist = ((b - j) * BLK).astype(F32)
        j_next = jnp.minimum(j + 1, b - 1)
        j_prev = jnp.maximum(j - 1, 0)
        p0 = past_softmax_stage(0, j, dist)
        s_ref[2] = score_stage(2, j)
        value_stage(3, j_prev, p_ref[...])
        p1 = past_softmax_stage(1, j, dist)
        s_ref[3] = score_stage(3, j)
        value_stage(0, j, p0)
        p2 = past_softmax_stage(2, j, dist)
        s_ref[0] = score_stage(0, j_next)
        value_stage(1, j, p1)
        p_ref[...] = past_softmax_stage(3, j, dist)
        s_ref[1] = score_stage(1, j_next)
        value_stage(2, j, p2)
        return carry

    lax.fori_loop(0, b, past_block, 0)
    value_stage(3, jnp.maximum(b - 1, 0), p_ref[...])
    for kv in range(N_KV):
        _store_heads(o_ref, kv, acc_ref[kv] / l_ref[kv])


def _moba_prompt_call(qT, katt, vT4, means, saug, slope_rows):
    n, _, t = qT.shape
    nb = t // BLK
    whole = lambda i, b: (0, 0, 0)
    return pl.pallas_call(
        functools.partial(_moba_prompt_kernel, nb=nb),
        grid=(n, nb),
        in_specs=[pl.BlockSpec((1, QD, BLK), lambda i, b: (i, 0, b)),
                  pl.BlockSpec((1, nb, BLK, N_KV * KAUG), lambda i, b: (i, 0, 0, 0)),
                  pl.BlockSpec((1, nb, KVD, BLK), lambda i, b: (i, 0, 0, 0)),
                  pl.BlockSpec((1, nb, KVD), lambda i, b: (i, 0, 0)),
                  pl.BlockSpec(saug.shape, whole),
                  pl.BlockSpec(slope_rows.shape, whole)],
        out_specs=pl.BlockSpec((1, QD, BLK), lambda i, b: (i, 0, b)),
        out_shape=jax.ShapeDtypeStruct((n, QD, t), BF16),
        scratch_shapes=[pltpu.VMEM((N_KV, KAUG, LANES), BF16),
                        pltpu.VMEM((N_KV, nb, LANES), F32),
                        pltpu.VMEM((N_KV, 1, LANES), F32),
                        pltpu.VMEM((N_KV, 1, LANES), F32),
                        pltpu.VMEM((N_KV, HD, LANES), F32),
                        pltpu.VMEM((N_KV, BLK, LANES), F32),
                        pltpu.VMEM((BLK, LANES), BF16)],
        compiler_params=_params("arbitrary", "arbitrary"),
        name="moba_prompt",
    )(qT, katt, vT4, means, saug, slope_rows)


def _prompt_layouts(qp, kp, vp, n, t):
    nb = t // BLK
    qT = qp.reshape(n, t, QD).transpose(0, 2, 1)
    key_off = (jnp.arange(n * t, dtype=jnp.int32) % BLK).astype(BF16)
    aug = jnp.zeros((n * t, N_KV, KAUG - HD), BF16)
    aug = aug.at[:, :, :N_SLOPE_PARTS].set(key_off[:, None, None])
    katt = jnp.concatenate([kp.astype(BF16).reshape(n * t, N_KV, HD), aug], axis=-1)
    katt = katt.reshape(n, nb, BLK, N_KV * KAUG)
    vT4 = vp.astype(BF16).reshape(n, nb, BLK, KVD).transpose(0, 1, 3, 2)
    return qT, katt, vT4


def _slope_aug(slopes_l2):
    rows = jnp.repeat(slopes_l2, BLK, axis=1)
    parts = []
    rest = rows
    for _ in range(N_SLOPE_PARTS):
        piece = rest.astype(BF16)
        parts.append(piece)
        rest = rest - piece.astype(F32)
    saug = jnp.zeros((N_KV, KAUG - HD, LANES), BF16)
    return saug.at[:, :N_SLOPE_PARTS, :].set(jnp.stack(parts, axis=1))


def _prompt_rows(per_head):
    return jnp.repeat(per_head.astype(F32), BLK, axis=1).reshape(N_KV, 1, LANES)


def _swa_prompt_attn(qp, kp, vp, slopes_l2, sink_l2, n, t):
    qT, katt, vT4 = _prompt_layouts(qp, kp, vp, n, t)
    oT = _swa_prompt_call(qT, katt, vT4, _slope_aug(slopes_l2), _prompt_rows(slopes_l2),
                          _prompt_rows(sink_l2))
    return oT.transpose(0, 2, 1).reshape(n * t, QD)


def _moba_prompt_attn(qp, kp, vp, means, slopes_l2, n, t):
    qT, katt, vT4 = _prompt_layouts(qp, kp, vp, n, t)
    oT = _moba_prompt_call(qT, katt, vT4, means, _slope_aug(slopes_l2), _prompt_rows(slopes_l2))
    return oT.transpose(0, 2, 1).reshape(n * t, QD)


def _new_token_scores(qb, knew, slope, tt, n_new):
    qf = qb.astype(F32)
    out = []
    for t in range(n_new):
        s = jnp.sum(qf * knew[t:t + 1, :], axis=1, keepdims=True)
        d = tt - float(t)
        out.append(jnp.where(d >= 0.0, s - slope * d, NEG))
    return out


def _fold_heads(o):
    rows = o.shape[0]
    step = rows // N_KV
    r = lax.broadcasted_iota(jnp.int32, (rows, KVD), 0) // step
    c = lax.broadcasted_iota(jnp.int32, (rows, KVD), 1) // HD
    o = jnp.where(r == c, o, 0.0)
    return o[0:step] + o[step:2 * step] + o[2 * step:3 * step] + o[3 * step:4 * step]


_NT = (((1,), (1,)), ((), ()))


def _swa_sample_kernel(qbd_ref, kc_ref, vc_ref, kn_ref, vn_ref, info_ref, o_ref, *, n_new):
    slope = info_ref[:, 0:1]
    tt = info_ref[:, 1:2]
    sink = info_ref[:, 2:3]
    wc = kc_ref.shape[2]
    cidx = lax.broadcasted_iota(jnp.int32, (1, wc), 1).astype(F32)
    delta = (float(wc) + tt) - cidx
    for s_i in range(qbd_ref.shape[0]):
        qb = qbd_ref[s_i]
        kn = kn_ref[s_i]
        vn = vn_ref[s_i]
        s = jnp.dot(qb, kc_ref[s_i].astype(BF16), preferred_element_type=F32)
        s = jnp.where(delta < float(WINDOW), s - slope * delta, NEG)
        s_new = _new_token_scores(qb, kn, slope, tt, n_new)
        m = jnp.maximum(jnp.max(s, axis=1, keepdims=True), sink)
        for sn in s_new:
            m = jnp.maximum(m, sn)
        p = jnp.exp2(s - m)
        denom = jnp.sum(p, axis=1, keepdims=True) + jnp.exp2(sink - m)
        o = lax.dot_general(p.astype(BF16), vc_ref[s_i].astype(BF16), _NT,
                            preferred_element_type=F32)
        for t, sn in enumerate(s_new):
            e = jnp.exp2(sn - m)
            denom = denom + e
            o = o + e * vn[t:t + 1, :]
        o_ref[s_i] = _fold_heads(o / denom)


def _swa_sample_call(qbd, kcT, vcT, kn, vn, info):
    nseq, rows, _ = qbd.shape
    wc = kcT.shape[2]
    n_new = kn.shape[1]
    sb = SWA_SEQS_PER_STEP
    assert nseq % sb == 0
    return pl.pallas_call(
        functools.partial(_swa_sample_kernel, n_new=n_new),
        grid=(nseq // sb,),
        in_specs=[pl.BlockSpec((sb, rows, KVD), lambda i: (i, 0, 0)),
                  pl.BlockSpec((sb, KVD, wc), lambda i: (i, 0, 0)),
                  pl.BlockSpec((sb, KVD, wc), lambda i: (i, 0, 0)),
                  pl.BlockSpec((sb, n_new, KVD), lambda i: (i, 0, 0)),
                  pl.BlockSpec((sb, n_new, KVD), lambda i: (i, 0, 0)),
                  pl.BlockSpec(info.shape, lambda i: (0, 0))],
        out_specs=pl.BlockSpec((sb, rows // N_KV, KVD), lambda i: (i, 0, 0)),
        out_shape=jax.ShapeDtypeStruct((nseq, rows // N_KV, KVD), F32),
        compiler_params=_params("arbitrary"),
        name="swa_sample",
    )(qbd, kcT, vcT, kn, vn, info)


def _moba_sample_kernel(pt_ref, *refs, n_new, n_blk, past_len):
    pps = MOBA_PAGES_PER_STEP
    k_refs = refs[:pps]
    v_refs = refs[pps:2 * pps]
    qbd_ref, kn_ref, vn_ref, info_ref, o_ref, m_ref, l_ref, oall_ref, meanT_ref = refs[2 * pps:]
    del pt_ref
    step = pl.program_id(1)
    slope = info_ref[:, 0:1]
    tt = info_ref[:, 1:2]
    qb = qbd_ref[0]
    rows = qb.shape[0]
    lane = lax.broadcasted_iota(jnp.int32, (rows, 128), 1)
    mlane = lax.broadcasted_iota(jnp.int32, (KVD, 128), 1)
    cidx = lax.broadcasted_iota(jnp.int32, (1, BLK), 1).astype(F32)
    pen0 = slope * (tt - cidx)

    @pl.when(step == 0)
    def _():
        m_ref[...] = jnp.zeros_like(m_ref)
        l_ref[...] = jnp.zeros_like(l_ref)
        meanT_ref[...] = jnp.zeros_like(meanT_ref)

    ppb = BLK // PAGE
    blocks = range(pps // ppb)
    scores = []
    for jb in blocks:
        kp = [k_refs[jb * ppb + i][0, 0] for i in range(ppb)]
        scores.append(jnp.dot(qb, jnp.concatenate(kp, axis=1).astype(BF16),
                              preferred_element_type=F32))
    means, m_new, l_new = meanT_ref[...], m_ref[...], l_ref[...]
    probs = []
    for jb in blocks:
        j = step * (pps // ppb) + jb
        ksum = k_refs[jb * ppb][0, 0]
        for i in range(1, ppb):
            ksum = ksum + k_refs[jb * ppb + i][0, 0]
        kmean = jnp.sum(ksum, axis=1, keepdims=True) * (1.0 / BLK)
        means = jnp.where(mlane == j, kmean, means)
        s = scores[jb] - pen0
        mj = jnp.max(s, axis=1, keepdims=True)
        p = jnp.exp2(s - mj)
        m_new = jnp.where(lane == j, mj, m_new)
        l_new = jnp.where(lane == j, jnp.sum(p, axis=1, keepdims=True), l_new)
        probs.append(p.astype(BF16))
    meanT_ref[...], m_ref[...], l_ref[...] = means, m_new, l_new
    for jb in blocks:
        j = step * (pps // ppb) + jb
        vp = [v_refs[jb * ppb + i][0, 0] for i in range(ppb)]
        oall_ref[j] = lax.dot_general(probs[jb], jnp.concatenate(vp, axis=1).astype(BF16), _NT,
                                      preferred_element_type=F32)

    @pl.when(step == pl.num_programs(1) - 1)
    def _():
        kn = kn_ref[0]
        vn = vn_ref[0]
        mean_hi, mean_lo = _split_bf16(meanT_ref[...])
        gate = (jnp.dot(qb, mean_hi, preferred_element_type=F32)
                + jnp.dot(qb, mean_lo, preferred_element_type=F32))[:, :n_blk]
        bl = lax.broadcasted_iota(jnp.int32, (rows, n_blk), 1)
        sel = jnp.zeros((rows, n_blk), jnp.bool_)
        for _ in range(min(TOPK, n_blk)):
            mx = jnp.max(gate, axis=1, keepdims=True)
            idx = jnp.min(jnp.where(gate == mx, bl, n_blk), axis=1, keepdims=True)
            hit = bl == idx
            sel = sel | hit
            gate = jnp.where(hit, NEG, gate)
        off = slope * (float(past_len) - bl.astype(F32) * float(BLK))
        mt = jnp.where(sel, m_ref[:, :n_blk] - off, NEG)
        s_new = _new_token_scores(qb, kn, slope, tt, n_new)
        m = jnp.max(mt, axis=1, keepdims=True)
        for sn in s_new:
            m = jnp.maximum(m, sn)
        w = jnp.where(sel, jnp.exp2(mt - m), 0.0)
        denom = jnp.sum(w * l_ref[:, :n_blk], axis=1, keepdims=True)
        o = jnp.zeros((rows, KVD), F32)
        for t, sn in enumerate(s_new):
            e = jnp.exp2(sn - m)
            denom = denom + e
            o = o + e * vn[t:t + 1, :]
        for j in range(n_blk):
            wj = jnp.sum(jnp.where(bl == j, w, 0.0), axis=1, keepdims=True)
            o = o + wj * oall_ref[j]
        o_ref[0] = _fold_heads(o / denom)


def _moba_sample_call(page_table, poolT_k, poolT_v, layer, qbd, kn, vn, info):
    nseq, rows, _ = qbd.shape
    n_pages = page_table.shape[1]
    n_new = kn.shape[1]
    pps = MOBA_PAGES_PER_STEP
    assert n_pages % pps == 0 and pps % (BLK // PAGE) == 0
    n_blk = n_pages * PAGE // BLK
    past_len = n_pages * PAGE
    assert past_len % BLK == 0 and n_new <= BLK and n_blk <= 128

    def page_spec(i):
        return pl.BlockSpec((1, 1, KVD, PAGE),
                            lambda s, g, pt: (layer, pt[s, g * pps + i], 0, 0))

    grid_spec = pltpu.PrefetchScalarGridSpec(
        num_scalar_prefetch=1,
        grid=(nseq, n_pages // pps),
        in_specs=([page_spec(i) for i in range(pps)] + [page_spec(i) for i in range(pps)]
                  + [pl.BlockSpec((1, rows, KVD), lambda s, g, pt: (s, 0, 0)),
                     pl.BlockSpec((1, n_new, KVD), lambda s, g, pt: (s, 0, 0)),
                     pl.BlockSpec((1, n_new, KVD), lambda s, g, pt: (s, 0, 0)),
                     pl.BlockSpec(info.shape, lambda s, g, pt: (0, 0))]),
        out_specs=pl.BlockSpec((1, rows // N_KV, KVD), lambda s, g, pt: (s, 0, 0)),
        scratch_shapes=[pltpu.VMEM((rows, 128), F32),
                        pltpu.VMEM((rows, 128), F32),
                        pltpu.VMEM((n_blk, rows, KVD), F32),
                        pltpu.VMEM((KVD, 128), F32)])
    return pl.pallas_call(
        functools.partial(_moba_sample_kernel, n_new=n_new, n_blk=n_blk, past_len=past_len),
        grid_spec=grid_spec,
        out_shape=jax.ShapeDtypeStruct((nseq, rows // N_KV, KVD), F32),
        compiler_params=_params("arbitrary", "arbitrary"),
        name="moba_sample",
    )(page_table, *([poolT_k] * pps), *([poolT_v] * pps), qbd, kn, vn, info)


def _block_diag_queries(q, nseq, n_new):
    qr = q.reshape(nseq, n_new, N_KV, GROUP, HD).transpose(0, 2, 1, 3, 4)
    eye = jnp.eye(N_KV, dtype=q.dtype)
    qbd = qr[:, :, :, :, None, :] * eye[None, :, None, None, :, None]
    return qbd.reshape(nseq, N_KV * n_new * GROUP, KVD)


def _unfold_sample_out(o, nseq, n_new):
    o = o.reshape(nseq, n_new, GROUP, N_KV, HD).transpose(0, 1, 3, 2, 4)
    return o.reshape(nseq * n_new, QD).astype(BF16)


def _rows_major_view(cache):
    lead = cache.shape[:-3]
    nd = len(lead)
    perm = tuple(range(nd)) + (nd + 1, nd + 2, nd)
    return cache.transpose(perm).reshape(lead + (KVD, cache.shape[-3]))


def _sample_info(slopes_l2, sink_l2, n_new):
    def per_row(per_head):
        return jnp.broadcast_to(per_head.astype(F32)[:, None, :], (N_KV, n_new, GROUP)).reshape(-1)
    tt = jnp.broadcast_to(jnp.arange(n_new, dtype=F32)[None, :, None],
                          (N_KV, n_new, GROUP)).reshape(-1)
    return jnp.stack([per_row(slopes_l2), tt, per_row(sink_l2), jnp.zeros_like(tt)], axis=1)


def _swa_sample_attn(qs, ks, vs, cache_k, cache_v, slopes_l2, sink_l2):
    nseq = cache_k.shape[0]
    n_new = qs.shape[0] // nseq
    o = _swa_sample_call(_block_diag_queries(qs, nseq, n_new),
                         _rows_major_view(cache_k), _rows_major_view(cache_v),
                         ks.reshape(nseq, n_new, KVD), vs.reshape(nseq, n_new, KVD),
                         _sample_info(slopes_l2, sink_l2, n_new))
    return _unfold_sample_out(o, nseq, n_new)


def _moba_sample_attn(qs, ks, vs, poolT_k, poolT_v, layer, page_table, slopes_l2):
    nseq = page_table.shape[0]
    n_new = qs.shape[0] // nseq
    o = _moba_sample_call(page_table, poolT_k, poolT_v, layer,
                          _block_diag_queries(qs, nseq, n_new),
                          ks.reshape(nseq, n_new, KVD), vs.reshape(nseq, n_new, KVD),
                          _sample_info(slopes_l2, jnp.zeros_like(slopes_l2), n_new))
    return _unfold_sample_out(o, nseq, n_new)


def kernel(x_prompt, x_sample, cache_swa_k, cache_swa_v, cache_moba_k, cache_moba_v, page_table,
           g_attn, w_qkv, g_q, g_k, sinks, w_o, g_mlp, w_up, w_down):
    n, t, _ = x_prompt.shape
    nseq, n_new, _ = x_sample.shape
    depth = w_qkv.shape[0]
    nb = t // BLK
    assert t % BLK == 0

    head = jnp.arange(1, N_HEADS + 1, dtype=F32)
    slopes_l2 = jnp.exp2(-8.0 * head / N_HEADS).reshape(N_KV, GROUP) * LOG2E

    wqkv_b = w_qkv.astype(BF16)
    wo_b = w_o.astype(BF16)
    wup_b = w_up.astype(BF16)
    wdn_b = w_down.astype(BF16)
    poolT_k = _rows_major_view(cache_moba_k)
    poolT_v = _rows_major_view(cache_moba_v)

    xp = x_prompt.reshape(n * t, D_MODEL)
    xs = x_sample.reshape(nseq * n_new, D_MODEL)
    swa_kp, swa_vp, swa_ks, swa_vs = [], [], [], []
    moba_kp, moba_vp, moba_ks, moba_vs = [], [], [], []
    for i in range(depth):
        is_swa = i % 2 == 0
        j = i // 2
        ga = g_attn[i].reshape(1, D_MODEL)
        gq = jnp.tile(g_q[i], N_KV).reshape(1, KVD)
        gk = jnp.tile(g_k[i], N_KV).reshape(1, KVD)
        res_p = _qkv_call(xp, ga, wqkv_b[i], gq, gk, with_means=not is_swa)
        qp, kp, vp = res_p[:3]
        qs, ks, vs = _qkv_call(xs, ga, wqkv_b[i], gq, gk, with_means=False)

        kp5 = kp.reshape(n, t, N_KV, HD)
        vp5 = vp.reshape(n, t, N_KV, HD)
        ks5 = ks.reshape(nseq, n_new, N_KV, HD)
        vs5 = vs.reshape(nseq, n_new, N_KV, HD)
        if is_swa:
            sink_l2 = sinks[j].reshape(N_KV, GROUP).astype(F32) * LOG2E
            attn_p = _swa_prompt_attn(qp, kp, vp, slopes_l2, sink_l2, n, t)
            attn_s = _swa_sample_attn(qs, ks, vs, cache_swa_k[j], cache_swa_v[j],
                                      slopes_l2, sink_l2)
            w = min(WINDOW, t)
            wc = cache_swa_k.shape[2]
            swa_kp.append(kp5[:, -w:])
            swa_vp.append(vp5[:, -w:])
            swa_ks.append(jnp.concatenate([cache_swa_k[j], ks5], axis=1)[:, -wc:])
            swa_vs.append(jnp.concatenate([cache_swa_v[j], vs5], axis=1)[:, -wc:])
        else:
            means = res_p[3].reshape(n, nb, KVD)
            attn_p = _moba_prompt_attn(qp, kp, vp, means, slopes_l2, n, t)
            attn_s = _moba_sample_attn(qs, ks, vs, poolT_k, poolT_v, j, page_table, slopes_l2)
            moba_kp.append(kp5)
            moba_vp.append(vp5)
            moba_ks.append(ks5)
            moba_vs.append(vs5)
        gm = g_mlp[i].reshape(1, D_MODEL)
        xp = _mlp_call(xp, attn_p, wo_b[i], gm, wup_b[i], wdn_b[i])
        xs = _mlp_call(xs, attn_s, wo_b[i], gm, wup_b[i], wdn_b[i])
    return (xp.reshape(n, t, D_MODEL), xs.reshape(nseq, n_new, D_MODEL),
            jnp.stack(swa_kp), jnp.stack(swa_vp), jnp.stack(swa_ks), jnp.stack(swa_vs),
            jnp.stack(moba_kp), jnp.stack(moba_vp), jnp.stack(moba_ks), jnp.stack(moba_vs))
```

```python
import functools
import math

import jax
import jax.numpy as jnp
from jax import lax
from jax.experimental import pallas as pl
from jax.experimental.pallas import tpu as pltpu

F32 = jnp.float32
BF16 = jnp.bfloat16

D_MODEL = 1024
N_HEADS = 16
N_KV = 4
GROUP = N_HEADS // N_KV
HD = D_MODEL // N_HEADS
QD = N_HEADS * HD
KVD = N_KV * HD
QKV_DIM = QD + 2 * KVD
D_FF = 4 * D_MODEL
WINDOW = 128
BLK = 256
TOPK = 3
PAGE = 128
EPS = 1e-6
NEG = -1e30
LOG2E = math.log2(math.e)

KAUG = 2 * HD
VAUG = HD + 16
N_SLOPE_PARTS = 3
LANES = GROUP * BLK

ROW_TILE = 512
FF_CHUNK = 1024
SWA_SEQS_PER_STEP = 8
MOBA_PAGES_PER_STEP = 16
VMEM_LIMIT = 56 * 1024 * 1024


def _params(*sem):
    return pltpu.CompilerParams(dimension_semantics=sem, vmem_limit_bytes=VMEM_LIMIT)


def _split_bf16(x):
    hi = x.astype(BF16)
    lo = (x - hi.astype(F32)).astype(BF16)
    return hi, lo


def _project_qkv(x_ref, g_ref, w_ref, gq_ref, gk_ref):
    x = x_ref[...]
    ms = jnp.mean(x * x, axis=-1, keepdims=True)
    xn = (x * lax.rsqrt(ms + EPS)) * g_ref[...]
    qkv = jnp.dot(xn.astype(BF16), w_ref[...], preferred_element_type=F32)

    r = lax.broadcasted_iota(jnp.int32, (KVD, KVD), 0) // HD
    c = lax.broadcasted_iota(jnp.int32, (KVD, KVD), 1) // HD
    seg = jnp.where(r == c, 1.0, 0.0).astype(BF16)

    def head_norm(z, gain):
        hi, lo = _split_bf16(z * z)
        tot = (jnp.dot(hi, seg, preferred_element_type=F32)
               + jnp.dot(lo, seg, preferred_element_type=F32))
        return (z * lax.rsqrt(tot * (1.0 / HD) + EPS)) * gain

    q = [head_norm(qkv[:, c * KVD:(c + 1) * KVD], gq_ref[...]) * (HD ** -0.5 * LOG2E)
         for c in range(QD // KVD)]
    k = head_norm(qkv[:, QD:QD + KVD], gk_ref[...])
    return q, k, qkv[:, QD + KVD:]


def _qkv_sample_kernel(x_ref, g_ref, w_ref, gq_ref, gk_ref, q_ref, k_ref, v_ref):
    q, k, v = _project_qkv(x_ref, g_ref, w_ref, gq_ref, gk_ref)
    for c, qc in enumerate(q):
        q_ref[:, c * KVD:(c + 1) * KVD] = qc.astype(q_ref.dtype)
    k_ref[...] = k
    v_ref[...] = v


def _qkv_prompt_kernel(x_ref, g_ref, w_ref, gq_ref, gk_ref,
                       qT_ref, kT_ref, vT_ref, katt_ref, vT4_ref, mean_ref):
    q, k, v = _project_qkv(x_ref, g_ref, w_ref, gq_ref, gk_ref)
    rows = k.shape[0]
    for c, qc in enumerate(q):
        qT_ref[0, c * KVD:(c + 1) * KVD, :] = qc.T.astype(qT_ref.dtype)
    kT_ref[0] = k.T
    vT = v.T
    vT_ref[0] = vT

    lane = lax.broadcasted_iota(jnp.int32, (rows, 2 * HD), 1)
    key_off = (lax.broadcasted_iota(jnp.int32, (rows, 2 * HD), 0) % BLK).astype(F32)
    aug = jnp.where((lane >= HD) & (lane < HD + N_SLOPE_PARTS), key_off, 0.0)
    groups = []
    for pair in range(N_KV // 2):
        two = k[:, pair * 2 * HD:(pair + 1) * 2 * HD]
        groups.append(jnp.where(lane < HD, two, aug))
        groups.append(jnp.where(lane < HD, pltpu.roll(two, HD, axis=1), aug))
    katt = jnp.concatenate(groups, axis=1).astype(katt_ref.dtype)
    ones_rows = jnp.where(lax.broadcasted_iota(jnp.int32, (VAUG - HD, BLK), 0) == 0, 1.0, 0.0)
    for b in range(rows // BLK):
        katt_ref[0, b] = katt[b * BLK:(b + 1) * BLK]
        vblk = vT[:, b * BLK:(b + 1) * BLK]
        vT4_ref[0, b] = jnp.concatenate(
            [piece for kv in range(N_KV) for piece in (vblk[kv * HD:(kv + 1) * HD], ones_rows)],
            axis=0).astype(vT4_ref.dtype)
        mean_ref[0, b:b + 1, :] = jnp.sum(
            k[b * BLK:(b + 1) * BLK], axis=0, keepdims=True) * (1.0 / BLK)


def _layer_spec(shape, layer):
    nd = len(shape)
    return pl.BlockSpec((None,) + tuple(shape), lambda *_: (layer,) + (0,) * nd,
                        pipeline_mode=pl.Buffered(1))


def _qkv_in_specs(x_spec, layer):
    return [x_spec,
            _layer_spec((1, D_MODEL), layer),
            _layer_spec((D_MODEL, QKV_DIM), layer),
            _layer_spec((1, KVD), layer),
            _layer_spec((1, KVD), layer)]


def _qkv_sample_call(x, g_attn, wqkv, gq, gk, layer):
    rows = x.shape[0]
    assert rows % ROW_TILE == 0
    return pl.pallas_call(
        _qkv_sample_kernel,
        grid=(rows // ROW_TILE,),
        in_specs=_qkv_in_specs(pl.BlockSpec((ROW_TILE, D_MODEL), lambda i: (i, 0)), layer),
        out_specs=[pl.BlockSpec((ROW_TILE, QD), lambda i: (i, 0)),
                   pl.BlockSpec((ROW_TILE, KVD), lambda i: (i, 0)),
                   pl.BlockSpec((ROW_TILE, KVD), lambda i: (i, 0))],
        out_shape=[jax.ShapeDtypeStruct((rows, QD), BF16),
                   jax.ShapeDtypeStruct((rows, KVD), F32),
                   jax.ShapeDtypeStruct((rows, KVD), F32)],
        compiler_params=_params("arbitrary"),
        name="qkv_sample",
    )(x, g_attn, wqkv, gq, gk)


def _qkv_prompt_call(x, g_attn, wqkv, gq, gk, layer, n, t):
    assert t % ROW_TILE == 0 and ROW_TILE % BLK == 0
    steps = t // ROW_TILE
    bpt = ROW_TILE // BLK
    nb = t // BLK
    return pl.pallas_call(
        _qkv_prompt_kernel,
        grid=(n, steps),
        in_specs=_qkv_in_specs(
            pl.BlockSpec((ROW_TILE, D_MODEL), lambda s, i: (s * steps + i, 0)), layer),
        out_specs=[pl.BlockSpec((1, QD, ROW_TILE), lambda s, i: (s, 0, i)),
                   pl.BlockSpec((1, KVD, ROW_TILE), lambda s, i: (s, 0, i)),
                   pl.BlockSpec((1, KVD, ROW_TILE), lambda s, i: (s, 0, i)),
                   pl.BlockSpec((1, bpt, BLK, N_KV * KAUG), lambda s, i: (s, i, 0, 0)),
                   pl.BlockSpec((1, bpt, N_KV * VAUG, BLK), lambda s, i: (s, i, 0, 0)),
                   pl.BlockSpec((1, bpt, KVD), lambda s, i: (s * steps + i, 0, 0))],
        out_shape=[jax.ShapeDtypeStruct((n, QD, t), BF16),
                   jax.ShapeDtypeStruct((n, KVD, t), F32),
                   jax.ShapeDtypeStruct((n, KVD, t), F32),
                   jax.ShapeDtypeStruct((n, nb, BLK, N_KV * KAUG), BF16),
                   jax.ShapeDtypeStruct((n, nb, N_KV * VAUG, BLK), BF16),
                   jax.ShapeDtypeStruct((n * steps, bpt, KVD), F32)],
        compiler_params=_params("arbitrary", "arbitrary"),
        name="qkv_prompt",
    )(x, g_attn, wqkv, gq, gk)


def _mlp_kernel(x_ref, a_ref, wo_ref, g_ref, wup_ref, wdn_ref, o_ref):
    x1 = x_ref[...] + jnp.dot(a_ref[...], wo_ref[...], preferred_element_type=F32)
    ms = jnp.mean(x1 * x1, axis=-1, keepdims=True)
    h = ((x1 * lax.rsqrt(ms + EPS)) * g_ref[...]).astype(BF16)
    acc = x1
    for c in range(D_FF // FF_CHUNK):
        u = jnp.dot(h, wup_ref[:, c * FF_CHUNK:(c + 1) * FF_CHUNK], preferred_element_type=F32)
        u = jnp.square(jnp.maximum(u, 0.0)).astype(BF16)
        acc = acc + jnp.dot(u, wdn_ref[c * FF_CHUNK:(c + 1) * FF_CHUNK, :],
                            preferred_element_type=F32)
    o_ref[...] = acc


def _mlp_call(x, attn, wo, g_mlp, wup, wdn, layer):
    rows = x.shape[0]
    assert rows % ROW_TILE == 0
    return pl.pallas_call(
        _mlp_kernel,
        grid=(rows // ROW_TILE,),
        in_specs=[pl.BlockSpec((ROW_TILE, D_MODEL), lambda i: (i, 0)),
                  pl.BlockSpec((ROW_TILE, QD), lambda i: (i, 0)),
                  _layer_spec((QD, D_MODEL), layer),
                  _layer_spec((1, D_MODEL), layer),
                  _layer_spec((D_MODEL, D_FF), layer),
                  _layer_spec((D_FF, D_MODEL), layer)],
        out_specs=pl.BlockSpec((ROW_TILE, D_MODEL), lambda i: (i, 0)),
        out_shape=jax.ShapeDtypeStruct((rows, D_MODEL), F32),
        compiler_params=_params("arbitrary"),
        name="oproj_mlp",
    )(x, attn, wo, g_mlp, wup, wdn)


def _aug_queries(qT_ref, slope, kv):
    qs = jnp.concatenate(
        [qT_ref[0, (kv * GROUP + g) * HD:(kv * GROUP + g + 1) * HD, :] for g in range(GROUP)],
        axis=1)
    row = lax.broadcasted_iota(jnp.int32, (KAUG - HD, LANES), 0)
    aug = jnp.zeros((KAUG - HD, LANES), F32)
    rest = slope
    for part in range(N_SLOPE_PARTS):
        piece = rest.astype(BF16).astype(F32)
        aug = jnp.where(row == part, piece, aug)
        rest = rest - piece
    return qs, jnp.concatenate([qs, aug.astype(BF16)], axis=0)


def _query_minus_key_offset():
    rr = lax.broadcasted_iota(jnp.int32, (1, LANES), 1) % BLK
    cc = lax.broadcasted_iota(jnp.int32, (BLK, 1), 0)
    return rr - cc


def _store_heads(o_ref, kv, o):
    for pair in range(GROUP // 2):
        two = jnp.concatenate([o[:, (2 * pair) * BLK:(2 * pair + 1) * BLK],
                               o[:, (2 * pair + 1) * BLK:(2 * pair + 2) * BLK]], axis=0)
        lane0 = (kv * GROUP + 2 * pair) * HD
        o_ref[0, :, lane0:lane0 + 2 * HD] = two.T.astype(o_ref.dtype)


def _swa_prompt_kernel(qT_ref, kc_ref, kp_ref, vc_ref, vp_ref, slope_ref, sink_ref, o_ref):
    b = pl.program_id(1)
    d = _query_minus_key_offset()
    cur_ok = (d >= 0) & (d < WINDOW)
    prev_ok = d < WINDOW - BLK
    prev_bias = jnp.where(b > 0, 0.0, NEG)
    rr = (lax.broadcasted_iota(jnp.int32, (1, LANES), 1) % BLK).astype(F32)
    scores = []
    for kv in range(N_KV):
        _, qaug = _aug_queries(qT_ref, slope_ref[kv], kv)
        scores.append((jnp.dot(kc_ref[0, 0, :, kv * KAUG:(kv + 1) * KAUG], qaug,
                               preferred_element_type=F32),
                       jnp.dot(kp_ref[0, 0, :, kv * KAUG:(kv + 1) * KAUG], qaug,
                               preferred_element_type=F32)))
    for kv in range(N_KV):
        slope = slope_ref[kv]
        sink = sink_ref[kv] + slope * rr
        prev_off = prev_bias - slope * float(BLK)
        s1 = jnp.where(cur_ok, scores[kv][0], NEG)
        s2 = jnp.where(prev_ok, scores[kv][1], NEG)
        m = jnp.maximum(jnp.maximum(jnp.max(s1, axis=0, keepdims=True),
                                    jnp.max(s2, axis=0, keepdims=True) + prev_off), sink)
        p1 = jnp.exp2(s1 - m).astype(BF16)
        p2 = jnp.exp2(s2 + (prev_off - m)).astype(BF16)
        acc = (jnp.dot(vc_ref[0, 0, kv * VAUG:(kv + 1) * VAUG, :], p1,
                       preferred_element_type=F32)
               + jnp.dot(vp_ref[0, 0, kv * VAUG:(kv + 1) * VAUG, :], p2,
                         preferred_element_type=F32))
        denom = acc[HD:HD + 1, :] + jnp.exp2(sink - m)
        _store_heads(o_ref, kv, acc[:HD, :] / denom)


def _swa_prompt_call(qT, katt, vT4, slope_rows, sink_rows):
    n, _, t = qT.shape
    nb = t // BLK
    cur = lambda i, b: (i, b, 0, 0)
    prev = lambda i, b: (i, jnp.maximum(b - 1, 0), 0, 0)
    whole = lambda i, b: (0, 0, 0)
    return pl.pallas_call(
        _swa_prompt_kernel,
        grid=(n, nb),
        in_specs=[pl.BlockSpec((1, QD, BLK), lambda i, b: (i, 0, b)),
                  pl.BlockSpec((1, 1, BLK, N_KV * KAUG), cur),
                  pl.BlockSpec((1, 1, BLK, N_KV * KAUG), prev),
                  pl.BlockSpec((1, 1, N_KV * VAUG, BLK), cur),
                  pl.BlockSpec((1, 1, N_KV * VAUG, BLK), prev),
                  pl.BlockSpec(slope_rows.shape, whole),
                  pl.BlockSpec(sink_rows.shape, whole)],
        out_specs=pl.BlockSpec((1, BLK, QD), lambda i, b: (i, b, 0)),
        out_shape=jax.ShapeDtypeStruct((n, t, QD), BF16),
        compiler_params=_params("arbitrary", "arbitrary"),
        name="swa_prompt",
    )(qT, katt, katt, vT4, vT4, slope_rows, sink_rows)


def _moba_prompt_kernel(qT_ref, k_ref, vT_ref, mean_ref, slope_ref, o_ref,
                        qaug_ref, sel_ref, m_ref, acc_ref, s_ref, p_ref, *, nb):
    assert N_KV == 4
    b = pl.program_id(1)
    causal = _query_minus_key_offset() >= 0
    jj = lax.broadcasted_iota(jnp.int32, (nb, LANES), 0)
    mean_hi, mean_lo = _split_bf16(mean_ref[0])

    def score_stage(kv, j):
        return jnp.dot(k_ref[0, j, :, kv * KAUG:(kv + 1) * KAUG], qaug_ref[kv],
                       preferred_element_type=F32)

    def value_stage(kv, j, p):
        acc_ref[kv] += jnp.dot(vT_ref[0, j, kv * VAUG:(kv + 1) * VAUG, :], p,
                               preferred_element_type=F32)

    def own_softmax_stage(kv, s):
        s = jnp.where(causal, s, NEG)
        m = jnp.max(s, axis=0, keepdims=True)
        m_ref[kv] = m
        acc_ref[kv] = jnp.zeros((VAUG, LANES), F32)
        return jnp.exp2(s - m).astype(BF16)

    def past_softmax_stage(kv, j, dist):
        s = s_ref[kv]
        bias = jnp.where(sel_ref[kv, pl.ds(j, 1), :] > 0.5, -dist * slope_ref[kv], NEG)
        m_old = m_ref[kv]
        m_new = jnp.maximum(m_old, jnp.max(s, axis=0, keepdims=True) + bias)
        acc_ref[kv] = jnp.exp2(m_old - m_new) * acc_ref[kv]
        m_ref[kv] = m_new
        return jnp.exp2(s + (bias - m_new)).astype(BF16)

    for kv in range(N_KV):
        qs, qaug = _aug_queries(qT_ref, slope_ref[kv], kv)
        qaug_ref[kv] = qaug

        gate = (jnp.dot(mean_hi[:, kv * HD:(kv + 1) * HD], qs, preferred_element_type=F32)
                + jnp.dot(mean_lo[:, kv * HD:(kv + 1) * HD], qs, preferred_element_type=F32))
        gate = jnp.where(jj < b, gate, NEG)
        sel = jnp.zeros((nb, LANES), F32)
        for _ in range(TOPK):
            mx = jnp.max(gate, axis=0, keepdims=True)
            idx = jnp.min(jnp.where(gate == mx, jj, nb), axis=0, keepdims=True)
            hit = jj == jnp.where(mx > 0.5 * NEG, idx, -1)
            sel = jnp.where(hit, 1.0, sel)
            gate = jnp.where(hit, NEG, gate)
        sel_ref[kv] = sel

    s_own = [score_stage(0, b), score_stage(1, b)]
    p_own = []
    for kv in range(N_KV):
        p_own.append(own_softmax_stage(kv, s_own[kv]))
        if kv + 2 < N_KV:
            s_own.append(score_stage(kv + 2, b))
        else:
            s_ref[kv + 2 - N_KV] = score_stage(kv + 2 - N_KV, 0)
        if kv >= 1:
            value_stage(kv - 1, b, p_own[kv - 1])
    value_stage(N_KV - 1, b, p_own[N_KV - 1])
    p_ref[...] = jnp.zeros_like(p_ref)

    def past_block(j, carry):
        dist = ((b - j) * BLK).astype(F32)
        j_next = jnp.minimum(j + 1, b - 1)
        j_prev = jnp.maximum(j - 1, 0)
        p0 = past_softmax_stage(0, j, dist)
        s_ref[2] = score_stage(2, j)
        value_stage(3, j_prev, p_ref[...])
        p1 = past_softmax_stage(1, j, dist)
        s_ref[3] = score_stage(3, j)
        value_stage(0, j, p0)
        p2 = past_softmax_stage(2, j, dist)
        s_ref[0] = score_stage(0, j_next)
        value_stage(1, j, p1)
        p_ref[...] = past_softmax_stage(3, j, dist)
        s_ref[1] = score_stage(1, j_next)
        value_stage(2, j, p2)
        return carry

    lax.fori_loop(0, b, past_block, 0)
    value_stage(3, jnp.maximum(b - 1, 0), p_ref[...])
    for kv in range(N_KV):
        _store_heads(o_ref, kv, acc_ref[kv, :HD, :] / acc_ref[kv, HD:HD + 1, :])


def _moba_prompt_call(qT, katt, vT4, means, slope_rows):
    n, _, t = qT.shape
    nb = t // BLK
    whole = lambda i, b: (0, 0, 0)
    return pl.pallas_call(
        functools.partial(_moba_prompt_kernel, nb=nb),
        grid=(n, nb),
        in_specs=[pl.BlockSpec((1, QD, BLK), lambda i, b: (i, 0, b)),
                  pl.BlockSpec((1, nb, BLK, N_KV * KAUG), lambda i, b: (i, 0, 0, 0)),
                  pl.BlockSpec((1, nb, N_KV * VAUG, BLK), lambda i, b: (i, 0, 0, 0)),
                  pl.BlockSpec((1, nb, KVD), lambda i, b: (i, 0, 0)),
                  pl.BlockSpec(slope_rows.shape, whole)],
        out_specs=pl.BlockSpec((1, BLK, QD), lambda i, b: (i, b, 0)),
        out_shape=jax.ShapeDtypeStruct((n, t, QD), BF16),
        scratch_shapes=[pltpu.VMEM((N_KV, KAUG, LANES), BF16),
                        pltpu.VMEM((N_KV, nb, LANES), F32),
                        pltpu.VMEM((N_KV, 1, LANES), F32),
                        pltpu.VMEM((N_KV, VAUG, LANES), F32),
                        pltpu.VMEM((N_KV, BLK, LANES), F32),
                        pltpu.VMEM((BLK, LANES), BF16)],
        compiler_params=_params("arbitrary", "arbitrary"),
        name="moba_prompt",
    )(qT, katt, vT4, means, slope_rows)


def _prompt_rows(per_head):
    return jnp.repeat(per_head.astype(F32), BLK, axis=1).reshape(N_KV, 1, LANES)


def _new_token_scores(qb, knew, slope, tt, n_new):
    qf = qb.astype(F32)
    out = []
    for t in range(n_new):
        s = jnp.sum(qf * knew[t:t + 1, :], axis=1, keepdims=True)
        d = tt - float(t)
        out.append(jnp.where(d >= 0.0, s - slope * d, NEG))
    return out


def _fold_heads(o):
    rows = o.shape[0]
    step = rows // N_KV
    r = lax.broadcasted_iota(jnp.int32, (rows, KVD), 0) // step
    c = lax.broadcasted_iota(jnp.int32, (rows, KVD), 1) // HD
    o = jnp.where(r == c, o, 0.0)
    return o[0:step] + o[step:2 * step] + o[2 * step:3 * step] + o[3 * step:4 * step]


_NT = (((1,), (1,)), ((), ()))


def _swa_sample_kernel(qbd_ref, kc_ref, vc_ref, kn_ref, vn_ref, info_ref, o_ref, *, n_new):
    slope = info_ref[:, 0:1]
    tt = info_ref[:, 1:2]
    sink = info_ref[:, 2:3]
    wc = kc_ref.shape[2]
    cidx = lax.broadcasted_iota(jnp.int32, (1, wc), 1).astype(F32)
    delta = (float(wc) + tt) - cidx
    seqs = range(qbd_ref.shape[0])
    scores = [jnp.dot(qbd_ref[s_i], kc_ref[s_i].astype(BF16), preferred_element_type=F32)
              for s_i in seqs]
    probs, tails = [], []
    for s_i in seqs:
        s = jnp.where(delta < float(WINDOW), scores[s_i] - slope * delta, NEG)
        s_new = _new_token_scores(qbd_ref[s_i], kn_ref[s_i], slope, tt, n_new)
        m = jnp.maximum(jnp.max(s, axis=1, keepdims=True), sink)
        for sn in s_new:
            m = jnp.maximum(m, sn)
        p = jnp.exp2(s - m)
        denom = jnp.sum(p, axis=1, keepdims=True) + jnp.exp2(sink - m)
        o_new = jnp.zeros((qbd_ref.shape[1], KVD), F32)
        for t, sn in enumerate(s_new):
            e = jnp.exp2(sn - m)
            denom = denom + e
            o_new = o_new + e * vn_ref[s_i][t:t + 1, :]
        probs.append(p.astype(BF16))
        tails.append((o_new, denom))
    for s_i in seqs:
        o = lax.dot_general(probs[s_i], vc_ref[s_i].astype(BF16), _NT,
                            preferred_element_type=F32)
        o_new, denom = tails[s_i]
        o_ref[s_i] = _fold_heads((o + o_new) / denom)


def _swa_sample_call(qbd, kcT, vcT, kn, vn, info):
    nseq, rows, _ = qbd.shape
    wc = kcT.shape[2]
    n_new = kn.shape[1]
    sb = SWA_SEQS_PER_STEP
    assert nseq % sb == 0
    return pl.pallas_call(
        functools.partial(_swa_sample_kernel, n_new=n_new),
        grid=(nseq // sb,),
        in_specs=[pl.BlockSpec((sb, rows, KVD), lambda i: (i, 0, 0)),
                  pl.BlockSpec((sb, KVD, wc), lambda i: (i, 0, 0)),
                  pl.BlockSpec((sb, KVD, wc), lambda i: (i, 0, 0)),
                  pl.BlockSpec((sb, n_new, KVD), lambda i: (i, 0, 0)),
                  pl.BlockSpec((sb, n_new, KVD), lambda i: (i, 0, 0)),
                  pl.BlockSpec(info.shape, lambda i: (0, 0))],
        out_specs=pl.BlockSpec((sb, rows // N_KV, KVD), lambda i: (i, 0, 0)),
        out_shape=jax.ShapeDtypeStruct((nseq, rows // N_KV, KVD), F32),
        compiler_params=_params("arbitrary"),
        name="swa_sample",
    )(qbd, kcT, vcT, kn, vn, info)


def _moba_sample_kernel(pt_ref, *refs, n_new, n_blk, past_len):
    pps = MOBA_PAGES_PER_STEP
    k_refs = refs[:pps]
    v_refs = refs[pps:2 * pps]
    qbd_ref, kn_ref, vn_ref, info_ref, o_ref, m_ref, l_ref, oall_ref, meanT_ref = refs[2 * pps:]
    del pt_ref
    step = pl.program_id(1)
    slope = info_ref[:, 0:1]
    tt = info_ref[:, 1:2]
    qb = qbd_ref[0]
    rows = qb.shape[0]
    lane = lax.broadcasted_iota(jnp.int32, (rows, 128), 1)
    mlane = lax.broadcasted_iota(jnp.int32, (KVD, 128), 1)
    cidx = lax.broadcasted_iota(jnp.int32, (1, BLK), 1).astype(F32)
    pen0 = slope * (tt - cidx)

    @pl.when(step == 0)
    def _():
        m_ref[...] = jnp.zeros_like(m_ref)
        l_ref[...] = jnp.zeros_like(l_ref)
        meanT_ref[...] = jnp.zeros_like(meanT_ref)

    ppb = BLK // PAGE
    blocks = range(pps // ppb)
    scores = []
    for jb in blocks:
        kp = [k_refs[jb * ppb + i][0, 0] for i in range(ppb)]
        scores.append(jnp.dot(qb, jnp.concatenate(kp, axis=1).astype(BF16),
                              preferred_element_type=F32))
    means, m_new, l_new = meanT_ref[...], m_ref[...], l_ref[...]
    probs = []
    for jb in blocks:
        j = step * (pps // ppb) + jb
        ksum = k_refs[jb * ppb][0, 0]
        for i in range(1, ppb):
            ksum = ksum + k_refs[jb * ppb + i][0, 0]
        kmean = jnp.sum(ksum, axis=1, keepdims=True) * (1.0 / BLK)
        means = jnp.where(mlane == j, kmean, means)
        s = scores[jb] - pen0
        mj = jnp.max(s, axis=1, keepdims=True)
        p = jnp.exp2(s - mj)
        m_new = jnp.where(lane == j, mj, m_new)
        l_new = jnp.where(lane == j, jnp.sum(p, axis=1, keepdims=True), l_new)
        probs.append(p.astype(BF16))
    meanT_ref[...], m_ref[...], l_ref[...] = means, m_new, l_new
    for jb in blocks:
        j = step * (pps // ppb) + jb
        vp = [v_refs[jb * ppb + i][0, 0] for i in range(ppb)]
        oall_ref[j] = lax.dot_general(probs[jb], jnp.concatenate(vp, axis=1).astype(BF16), _NT,
                                      preferred_element_type=F32)

    @pl.when(step == pl.num_programs(1) - 1)
    def _():
        kn = kn_ref[0]
        vn = vn_ref[0]
        mean_hi, mean_lo = _split_bf16(meanT_ref[...])
        gate = (jnp.dot(qb, mean_hi, preferred_element_type=F32)
                + jnp.dot(qb, mean_lo, preferred_element_type=F32))[:, :n_blk]
        bl = lax.broadcasted_iota(jnp.int32, (rows, n_blk), 1)
        sel = jnp.zeros((rows, n_blk), jnp.bool_)
        for _ in range(min(TOPK, n_blk)):
            mx = jnp.max(gate, axis=1, keepdims=True)
            idx = jnp.min(jnp.where(gate == mx, bl, n_blk), axis=1, keepdims=True)
            hit = bl == idx
            sel = sel | hit
            gate = jnp.where(hit, NEG, gate)
        off = slope * (float(past_len) - bl.astype(F32) * float(BLK))
        mt = jnp.where(sel, m_ref[:, :n_blk] - off, NEG)
        s_new = _new_token_scores(qb, kn, slope, tt, n_new)
        m = jnp.max(mt, axis=1, keepdims=True)
        for sn in s_new:
            m = jnp.maximum(m, sn)
        w = jnp.where(sel, jnp.exp2(mt - m), 0.0)
        denom = jnp.sum(w * l_ref[:, :n_blk], axis=1, keepdims=True)
        o = jnp.zeros((rows, KVD), F32)
        for t, sn in enumerate(s_new):
            e = jnp.exp2(sn - m)
            denom = denom + e
            o = o + e * vn[t:t + 1, :]
        for j in range(n_blk):
            wj = jnp.sum(jnp.where(bl == j, w, 0.0), axis=1, keepdims=True)
            o = o + wj * oall_ref[j]
        o_ref[0] = _fold_heads(o / denom)


def _moba_sample_call(page_table, poolT_k, poolT_v, layer, qbd, kn, vn, info):
    nseq, rows, _ = qbd.shape
    n_pages = page_table.shape[1]
    n_new = kn.shape[1]
    pps = MOBA_PAGES_PER_STEP
    assert n_pages % pps == 0 and pps % (BLK // PAGE) == 0
    n_blk = n_pages * PAGE // BLK
    past_len = n_pages * PAGE
    assert past_len % BLK == 0 and n_new <= BLK and n_blk <= 128

    def page_spec(i):
        return pl.BlockSpec((1, 1, KVD, PAGE),
                            lambda s, g, pt: (layer, pt[s, g * pps + i], 0, 0))

    grid_spec = pltpu.PrefetchScalarGridSpec(
        num_scalar_prefetch=1,
        grid=(nseq, n_pages // pps),
        in_specs=([page_spec(i) for i in range(pps)] + [page_spec(i) for i in range(pps)]
                  + [pl.BlockSpec((1, rows, KVD), lambda s, g, pt: (s, 0, 0)),
                     pl.BlockSpec((1, n_new, KVD), lambda s, g, pt: (s, 0, 0)),
                     pl.BlockSpec((1, n_new, KVD), lambda s, g, pt: (s, 0, 0)),
                     pl.BlockSpec(info.shape, lambda s, g, pt: (0, 0))]),
        out_specs=pl.BlockSpec((1, rows // N_KV, KVD), lambda s, g, pt: (s, 0, 0)),
        scratch_shapes=[pltpu.VMEM((rows, 128), F32),
                        pltpu.VMEM((rows, 128), F32),
                        pltpu.VMEM((n_blk, rows, KVD), F32),
                        pltpu.VMEM((KVD, 128), F32)])
    return pl.pallas_call(
        functools.partial(_moba_sample_kernel, n_new=n_new, n_blk=n_blk, past_len=past_len),
        grid_spec=grid_spec,
        out_shape=jax.ShapeDtypeStruct((nseq, rows // N_KV, KVD), F32),
        compiler_params=_params("arbitrary", "arbitrary"),
        name="moba_sample",
    )(page_table, *([poolT_k] * pps), *([poolT_v] * pps), qbd, kn, vn, info)


def _block_diag_queries(q, nseq, n_new):
    qr = q.reshape(nseq, n_new, N_KV, GROUP, HD).transpose(0, 2, 1, 3, 4)
    eye = jnp.eye(N_KV, dtype=q.dtype)
    qbd = qr[:, :, :, :, None, :] * eye[None, :, None, None, :, None]
    return qbd.reshape(nseq, N_KV * n_new * GROUP, KVD)


def _unfold_sample_out(o, nseq, n_new):
    o = o.reshape(nseq, n_new, GROUP, N_KV, HD).transpose(0, 1, 3, 2, 4)
    return o.reshape(nseq * n_new, QD).astype(BF16)


def _rows_major_view(cache):
    lead = cache.shape[:-3]
    nd = len(lead)
    perm = tuple(range(nd)) + (nd + 1, nd + 2, nd)
    return cache.transpose(perm).reshape(lead + (KVD, cache.shape[-3]))


def _sample_info(slopes_l2, sink_l2, n_new):
    def per_row(per_head):
        return jnp.broadcast_to(per_head.astype(F32)[:, None, :], (N_KV, n_new, GROUP)).reshape(-1)
    tt = jnp.broadcast_to(jnp.arange(n_new, dtype=F32)[None, :, None],
                          (N_KV, n_new, GROUP)).reshape(-1)
    return jnp.stack([per_row(slopes_l2), tt, per_row(sink_l2), jnp.zeros_like(tt)], axis=1)


def _swa_sample_attn(qs, ks, vs, cache_k, cache_v, slopes_l2, sink_l2):
    nseq = cache_k.shape[0]
    n_new = qs.shape[0] // nseq
    o = _swa_sample_call(_block_diag_queries(qs, nseq, n_new),
                         _rows_major_view(cache_k), _rows_major_view(cache_v),
                         ks.reshape(nseq, n_new, KVD), vs.reshape(nseq, n_new, KVD),
                         _sample_info(slopes_l2, sink_l2, n_new))
    return _unfold_sample_out(o, nseq, n_new)


def _moba_sample_attn(qs, ks, vs, poolT_k, poolT_v, layer, page_table, slopes_l2):
    nseq = page_table.shape[0]
    n_new = qs.shape[0] // nseq
    o = _moba_sample_call(page_table, poolT_k, poolT_v, layer,
                          _block_diag_queries(qs, nseq, n_new),
                          ks.reshape(nseq, n_new, KVD), vs.reshape(nseq, n_new, KVD),
                          _sample_info(slopes_l2, jnp.zeros_like(slopes_l2), n_new))
    return _unfold_sample_out(o, nseq, n_new)


def kernel(x_prompt, x_sample, cache_swa_k, cache_swa_v, cache_moba_k, cache_moba_v, page_table,
           g_attn, w_qkv, g_q, g_k, sinks, w_o, g_mlp, w_up, w_down):
    n, t, _ = x_prompt.shape
    nseq, n_new, _ = x_sample.shape
    depth = w_qkv.shape[0]
    nb = t // BLK
    assert t % BLK == 0

    head = jnp.arange(1, N_HEADS + 1, dtype=F32)
    slopes_l2 = jnp.exp2(-8.0 * head / N_HEADS).reshape(N_KV, GROUP) * LOG2E

    wqkv_b = w_qkv.astype(BF16)
    wo_b = w_o.astype(BF16)
    wup_b = w_up.astype(BF16)
    wdn_b = w_down.astype(BF16)
    poolT_k = _rows_major_view(cache_moba_k)
    poolT_v = _rows_major_view(cache_moba_v)

    ga = g_attn.reshape(depth, 1, D_MODEL)
    gm = g_mlp.reshape(depth, 1, D_MODEL)
    gq = jnp.tile(g_q, (1, N_KV)).reshape(depth, 1, KVD)
    gk = jnp.tile(g_k, (1, N_KV)).reshape(depth, 1, KVD)
    slope_rows = _prompt_rows(slopes_l2)

    def rows_first(xT):
        return xT.reshape(n, N_KV, HD, xT.shape[-1]).transpose(0, 3, 1, 2)

    xp = x_prompt.reshape(n * t, D_MODEL)
    xs = x_sample.reshape(nseq * n_new, D_MODEL)
    swa_kp, swa_vp, swa_ks, swa_vs = [], [], [], []
    moba_kp, moba_vp, moba_ks, moba_vs = [], [], [], []
    for i in range(depth):
        is_swa = i % 2 == 0
        j = i // 2
        qT, kT, vT, katt, vT4, means = _qkv_prompt_call(xp, ga, wqkv_b, gq, gk, i, n, t)
        qs, ks, vs = _qkv_sample_call(xs, ga, wqkv_b, gq, gk, i)
        ks5 = ks.reshape(nseq, n_new, N_KV, HD)
        vs5 = vs.reshape(nseq, n_new, N_KV, HD)
        if is_swa:
            sink_l2 = sinks[j].reshape(N_KV, GROUP).astype(F32) * LOG2E
            attn_p = _swa_prompt_call(qT, katt, vT4, slope_rows, _prompt_rows(sink_l2))
            attn_s = _swa_sample_attn(qs, ks, vs, cache_swa_k[j], cache_swa_v[j],
                                      slopes_l2, sink_l2)
            w = min(WINDOW, t)
            wc = cache_swa_k.shape[2]
            swa_kp.append(rows_first(kT[:, :, t - w:]))
            swa_vp.append(rows_first(vT[:, :, t - w:]))
            swa_ks.append(jnp.concatenate([cache_swa_k[j], ks5], axis=1)[:, -wc:])
            swa_vs.append(jnp.concatenate([cache_swa_v[j], vs5], axis=1)[:, -wc:])
        else:
            attn_p = _moba_prompt_call(qT, katt, vT4, means.reshape(n, nb, KVD), slope_rows)
            attn_s = _moba_sample_attn(qs, ks, vs, poolT_k, poolT_v, j, page_table, slopes_l2)
            moba_kp.append(rows_first(kT))
            moba_vp.append(rows_first(vT))
            moba_ks.append(ks5)
            moba_vs.append(vs5)
        xp = _mlp_call(xp, attn_p.reshape(n * t, QD), wo_b, gm, wup_b, wdn_b, i)
        xs = _mlp_call(xs, attn_s, wo_b, gm, wup_b, wdn_b, i)
    return (xp.reshape(n, t, D_MODEL), xs.reshape(nseq, n_new, D_MODEL),
            jnp.stack(swa_kp), jnp.stack(swa_vp), jnp.stack(swa_ks), jnp.stack(swa_vs),
            jnp.stack(moba_kp), jnp.stack(moba_vp), jnp.stack(moba_ks), jnp.stack(moba_vs))
```

```python
import functools
import math

import jax
import jax.numpy as jnp
from jax import lax
from jax.experimental import pallas as pl
from jax.experimental.pallas import tpu as pltpu

F32 = jnp.float32
BF16 = jnp.bfloat16

D_MODEL = 1024
N_HEADS = 16
N_KV = 4
GROUP = N_HEADS // N_KV
HD = D_MODEL // N_HEADS
QD = N_HEADS * HD
KVD = N_KV * HD
QKV_DIM = QD + 2 * KVD
D_FF = 4 * D_MODEL
WINDOW = 128
BLK = 256
TOPK = 3
PAGE = 128
EPS = 1e-6
NEG = -1e30
LOG2E = math.log2(math.e)

KAUG = 2 * HD
VAUG = HD + 16
N_SLOPE_PARTS = 3
LANES = GROUP * BLK

ROW_TILE = 512
FF_CHUNK = 1024
SWA_SEQS_PER_STEP = 8
MOBA_PAGES_PER_STEP = 16
VMEM_LIMIT = 56 * 1024 * 1024


def _params(*sem):
    return pltpu.CompilerParams(dimension_semantics=sem, vmem_limit_bytes=VMEM_LIMIT)


def _split_bf16(x):
    hi = x.astype(BF16)
    lo = (x - hi.astype(F32)).astype(BF16)
    return hi, lo


def _project_qkv(x_ref, g_ref, w_ref, gq_ref, gk_ref):
    x = x_ref[...]
    ms = jnp.mean(x * x, axis=-1, keepdims=True)
    xn = (x * lax.rsqrt(ms + EPS)) * g_ref[...]
    qkv = jnp.dot(xn.astype(BF16), w_ref[...], preferred_element_type=F32)

    r = lax.broadcasted_iota(jnp.int32, (KVD, KVD), 0) // HD
    c = lax.broadcasted_iota(jnp.int32, (KVD, KVD), 1) // HD
    seg = jnp.where(r == c, 1.0, 0.0).astype(BF16)

    def head_norm(z, gain):
        hi, lo = _split_bf16(z * z)
        tot = (jnp.dot(hi, seg, preferred_element_type=F32)
               + jnp.dot(lo, seg, preferred_element_type=F32))
        return (z * lax.rsqrt(tot * (1.0 / HD) + EPS)) * gain

    q = [head_norm(qkv[:, c * KVD:(c + 1) * KVD], gq_ref[...]) * (HD ** -0.5 * LOG2E)
         for c in range(QD // KVD)]
    k = head_norm(qkv[:, QD:QD + KVD], gk_ref[...])
    return q, k, qkv[:, QD + KVD:]


def _qkv_sample_kernel(x_ref, g_ref, w_ref, gq_ref, gk_ref, q_ref, k_ref, v_ref):
    q, k, v = _project_qkv(x_ref, g_ref, w_ref, gq_ref, gk_ref)
    for c, qc in enumerate(q):
        q_ref[:, c * KVD:(c + 1) * KVD] = qc.astype(q_ref.dtype)
    k_ref[...] = k
    v_ref[...] = v


def _qkv_prompt_kernel(x_ref, g_ref, w_ref, gq_ref, gk_ref,
                       qT_ref, kT_ref, vT_ref, katt_ref, vT4_ref, mean_ref):
    q, k, v = _project_qkv(x_ref, g_ref, w_ref, gq_ref, gk_ref)
    rows = k.shape[0]
    for c, qc in enumerate(q):
        qT_ref[0, c * KVD:(c + 1) * KVD, :] = qc.T.astype(qT_ref.dtype)
    kT_ref[0] = k.T
    vT = v.T
    vT_ref[0] = vT

    lane = lax.broadcasted_iota(jnp.int32, (rows, 2 * HD), 1)
    key_off = (lax.broadcasted_iota(jnp.int32, (rows, 2 * HD), 0) % BLK).astype(F32)
    aug = jnp.where((lane >= HD) & (lane < HD + N_SLOPE_PARTS), key_off, 0.0)
    groups = []
    for pair in range(N_KV // 2):
        two = k[:, pair * 2 * HD:(pair + 1) * 2 * HD]
        groups.append(jnp.where(lane < HD, two, aug))
        groups.append(jnp.where(lane < HD, pltpu.roll(two, HD, axis=1), aug))
    katt = jnp.concatenate(groups, axis=1).astype(katt_ref.dtype)
    ones_rows = jnp.where(lax.broadcasted_iota(jnp.int32, (VAUG - HD, BLK), 0) == 0, 1.0, 0.0)
    for b in range(rows // BLK):
        katt_ref[0, b] = katt[b * BLK:(b + 1) * BLK]
        vblk = vT[:, b * BLK:(b + 1) * BLK]
        vT4_ref[0, b] = jnp.concatenate(
            [piece for kv in range(N_KV) for piece in (vblk[kv * HD:(kv + 1) * HD], ones_rows)],
            axis=0).astype(vT4_ref.dtype)
        mean_ref[0, b:b + 1, :] = jnp.sum(
            k[b * BLK:(b + 1) * BLK], axis=0, keepdims=True) * (1.0 / BLK)


def _layer_spec(shape, layer):
    nd = len(shape)
    return pl.BlockSpec((None,) + tuple(shape), lambda *_: (layer,) + (0,) * nd,
                        pipeline_mode=pl.Buffered(1))


def _qkv_in_specs(x_spec, layer):
    return [x_spec,
            _layer_spec((1, D_MODEL), layer),
            _layer_spec((D_MODEL, QKV_DIM), layer),
            _layer_spec((1, KVD), layer),
            _layer_spec((1, KVD), layer)]


def _qkv_sample_call(x, g_attn, wqkv, gq, gk, layer):
    rows = x.shape[0]
    assert rows % ROW_TILE == 0
    return pl.pallas_call(
        _qkv_sample_kernel,
        grid=(rows // ROW_TILE,),
        in_specs=_qkv_in_specs(pl.BlockSpec((ROW_TILE, D_MODEL), lambda i: (i, 0)), layer),
        out_specs=[pl.BlockSpec((ROW_TILE, QD), lambda i: (i, 0)),
                   pl.BlockSpec((ROW_TILE, KVD), lambda i: (i, 0)),
                   pl.BlockSpec((ROW_TILE, KVD), lambda i: (i, 0))],
        out_shape=[jax.ShapeDtypeStruct((rows, QD), BF16),
                   jax.ShapeDtypeStruct((rows, KVD), F32),
                   jax.ShapeDtypeStruct((rows, KVD), F32)],
        compiler_params=_params("arbitrary"),
        name="qkv_sample",
    )(x, g_attn, wqkv, gq, gk)


def _qkv_prompt_call(x, g_attn, wqkv, gq, gk, layer, n, t):
    assert t % ROW_TILE == 0 and ROW_TILE % BLK == 0
    steps = t // ROW_TILE
    bpt = ROW_TILE // BLK
    nb = t // BLK
    return pl.pallas_call(
        _qkv_prompt_kernel,
        grid=(n, steps),
        in_specs=_qkv_in_specs(
            pl.BlockSpec((ROW_TILE, D_MODEL), lambda s, i: (s * steps + i, 0)), layer),
        out_specs=[pl.BlockSpec((1, QD, ROW_TILE), lambda s, i: (s, 0, i)),
                   pl.BlockSpec((1, KVD, ROW_TILE), lambda s, i: (s, 0, i)),
                   pl.BlockSpec((1, KVD, ROW_TILE), lambda s, i: (s, 0, i)),
                   pl.BlockSpec((1, bpt, BLK, N_KV * KAUG), lambda s, i: (s, i, 0, 0)),
                   pl.BlockSpec((1, bpt, N_KV * VAUG, BLK), lambda s, i: (s, i, 0, 0)),
                   pl.BlockSpec((1, bpt, KVD), lambda s, i: (s * steps + i, 0, 0))],
        out_shape=[jax.ShapeDtypeStruct((n, QD, t), BF16),
                   jax.ShapeDtypeStruct((n, KVD, t), F32),
                   jax.ShapeDtypeStruct((n, KVD, t), F32),
                   jax.ShapeDtypeStruct((n, nb, BLK, N_KV * KAUG), BF16),
                   jax.ShapeDtypeStruct((n, nb, N_KV * VAUG, BLK), BF16),
                   jax.ShapeDtypeStruct((n * steps, bpt, KVD), F32)],
        compiler_params=_params("arbitrary", "arbitrary"),
        name="qkv_prompt",
    )(x, g_attn, wqkv, gq, gk)


def _mlp_kernel(x_ref, a_ref, wo_ref, g_ref, wup_ref, wdn_ref, o_ref):
    x1 = x_ref[...] + jnp.dot(a_ref[...], wo_ref[...], preferred_element_type=F32)
    ms = jnp.mean(x1 * x1, axis=-1, keepdims=True)
    h = ((x1 * lax.rsqrt(ms + EPS)) * g_ref[...]).astype(BF16)
    acc = x1
    for c in range(D_FF // FF_CHUNK):
        u = jnp.dot(h, wup_ref[:, c * FF_CHUNK:(c + 1) * FF_CHUNK], preferred_element_type=F32)
        u = jnp.square(jnp.maximum(u, 0.0)).astype(BF16)
        acc = acc + jnp.dot(u, wdn_ref[c * FF_CHUNK:(c + 1) * FF_CHUNK, :],
                            preferred_element_type=F32)
    o_ref[...] = acc


def _mlp_call(x, attn, wo, g_mlp, wup, wdn, layer):
    rows = x.shape[0]
    assert rows % ROW_TILE == 0
    return pl.pallas_call(
        _mlp_kernel,
        grid=(rows // ROW_TILE,),
        in_specs=[pl.BlockSpec((ROW_TILE, D_MODEL), lambda i: (i, 0)),
                  pl.BlockSpec((ROW_TILE, QD), lambda i: (i, 0)),
                  _layer_spec((QD, D_MODEL), layer),
                  _layer_spec((1, D_MODEL), layer),
                  _layer_spec((D_MODEL, D_FF), layer),
                  _layer_spec((D_FF, D_MODEL), layer)],
        out_specs=pl.BlockSpec((ROW_TILE, D_MODEL), lambda i: (i, 0)),
        out_shape=jax.ShapeDtypeStruct((rows, D_MODEL), F32),
        compiler_params=_params("arbitrary"),
        name="oproj_mlp",
    )(x, attn, wo, g_mlp, wup, wdn)


def _aug_queries(qT_ref, slope, kv):
    qs = jnp.concatenate(
        [qT_ref[0, (kv * GROUP + g) * HD:(kv * GROUP + g + 1) * HD, :] for g in range(GROUP)],
        axis=1)
    row = lax.broadcasted_iota(jnp.int32, (KAUG - HD, LANES), 0)
    aug = jnp.zeros((KAUG - HD, LANES), F32)
    rest = slope
    for part in range(N_SLOPE_PARTS):
        piece = rest.astype(BF16).astype(F32)
        aug = jnp.where(row == part, piece, aug)
        rest = rest - piece
    return qs, jnp.concatenate([qs, aug.astype(BF16)], axis=0)


def _query_minus_key_offset():
    rr = lax.broadcasted_iota(jnp.int32, (1, LANES), 1) % BLK
    cc = lax.broadcasted_iota(jnp.int32, (BLK, 1), 0)
    return rr - cc


def _store_heads(o_ref, kv, o):
    for pair in range(GROUP // 2):
        two = jnp.concatenate([o[:, (2 * pair) * BLK:(2 * pair + 1) * BLK],
                               o[:, (2 * pair + 1) * BLK:(2 * pair + 2) * BLK]], axis=0)
        lane0 = (kv * GROUP + 2 * pair) * HD
        o_ref[0, :, lane0:lane0 + 2 * HD] = two.T.astype(o_ref.dtype)


def _swa_prompt_kernel(qT_ref, kc_ref, kp_ref, vc_ref, vp_ref, slope_ref, sink_ref, o_ref):
    b = pl.program_id(1)
    d = _query_minus_key_offset()
    cur_ok = (d >= 0) & (d < WINDOW)
    prev_ok = d < WINDOW - BLK
    prev_bias = jnp.where(b > 0, 0.0, NEG)
    rr = (lax.broadcasted_iota(jnp.int32, (1, LANES), 1) % BLK).astype(F32)
    scores = []
    for kv in range(N_KV):
        _, qaug = _aug_queries(qT_ref, slope_ref[kv], kv)
        scores.append((jnp.dot(kc_ref[0, 0, :, kv * KAUG:(kv + 1) * KAUG], qaug,
                               preferred_element_type=F32),
                       jnp.dot(kp_ref[0, 0, :, kv * KAUG:(kv + 1) * KAUG], qaug,
                               preferred_element_type=F32)))
    for kv in range(N_KV):
        slope = slope_ref[kv]
        sink = sink_ref[kv] + slope * rr
        prev_off = prev_bias - slope * float(BLK)
        s1 = jnp.where(cur_ok, scores[kv][0], NEG)
        s2 = jnp.where(prev_ok, scores[kv][1], NEG)
        m = jnp.maximum(jnp.maximum(jnp.max(s1, axis=0, keepdims=True),
                                    jnp.max(s2, axis=0, keepdims=True) + prev_off), sink)
        p1 = jnp.exp2(s1 - m).astype(BF16)
        p2 = jnp.exp2(s2 + (prev_off - m)).astype(BF16)
        acc = (jnp.dot(vc_ref[0, 0, kv * VAUG:(kv + 1) * VAUG, :], p1,
                       preferred_element_type=F32)
               + jnp.dot(vp_ref[0, 0, kv * VAUG:(kv + 1) * VAUG, :], p2,
                         preferred_element_type=F32))
        denom = acc[HD:HD + 1, :] + jnp.exp2(sink - m)
        _store_heads(o_ref, kv, acc[:HD, :] / denom)


def _swa_prompt_call(qT, katt, vT4, slope_rows, sink_rows):
    n, _, t = qT.shape
    nb = t // BLK
    cur = lambda i, b: (i, b, 0, 0)
    prev = lambda i, b: (i, jnp.maximum(b - 1, 0), 0, 0)
    whole = lambda i, b: (0, 0, 0)
    return pl.pallas_call(
        _swa_prompt_kernel,
        grid=(n, nb),
        in_specs=[pl.BlockSpec((1, QD, BLK), lambda i, b: (i, 0, b)),
                  pl.BlockSpec((1, 1, BLK, N_KV * KAUG), cur),
                  pl.BlockSpec((1, 1, BLK, N_KV * KAUG), prev),
                  pl.BlockSpec((1, 1, N_KV * VAUG, BLK), cur),
                  pl.BlockSpec((1, 1, N_KV * VAUG, BLK), prev),
                  pl.BlockSpec(slope_rows.shape, whole),
                  pl.BlockSpec(sink_rows.shape, whole)],
        out_specs=pl.BlockSpec((1, BLK, QD), lambda i, b: (i, b, 0)),
        out_shape=jax.ShapeDtypeStruct((n, t, QD), BF16),
        compiler_params=_params("arbitrary", "arbitrary"),
        name="swa_prompt",
    )(qT, katt, katt, vT4, vT4, slope_rows, sink_rows)


def _moba_prompt_kernel(qT_ref, k_ref, vT_ref, mean_ref, slope_ref, o_ref,
                        qaug_ref, sel_ref, m_ref, acc_ref, s_ref, p_ref, *, nb):
    assert N_KV == 4
    b = pl.program_id(1)
    causal = _query_minus_key_offset() >= 0
    jj = lax.broadcasted_iota(jnp.int32, (nb, LANES), 0)
    mean_hi, mean_lo = _split_bf16(mean_ref[0])

    def score_stage(kv, j):
        return jnp.dot(k_ref[0, j, :, kv * KAUG:(kv + 1) * KAUG], qaug_ref[kv],
                       preferred_element_type=F32)

    def value_stage(kv, j, p):
        acc_ref[kv] += jnp.dot(vT_ref[0, j, kv * VAUG:(kv + 1) * VAUG, :], p,
                               preferred_element_type=F32)

    def own_softmax_stage(kv, s):
        s = jnp.where(causal, s, NEG)
        m = jnp.max(s, axis=0, keepdims=True)
        m_ref[kv] = m
        acc_ref[kv] = jnp.zeros((VAUG, LANES), F32)
        return jnp.exp2(s - m).astype(BF16)

    def past_softmax_stage(kv, j, dist):
        s = s_ref[kv]
        bias = jnp.where(sel_ref[kv, pl.ds(j, 1), :] > 0.5, -dist * slope_ref[kv], NEG)
        m_old = m_ref[kv]
        m_new = jnp.maximum(m_old, jnp.max(s, axis=0, keepdims=True) + bias)
        acc_ref[kv] = jnp.exp2(m_old - m_new) * acc_ref[kv]
        m_ref[kv] = m_new
        return jnp.exp2(s + (bias - m_new)).astype(BF16)

    for kv in range(N_KV):
        qs, qaug = _aug_queries(qT_ref, slope_ref[kv], kv)
        qaug_ref[kv] = qaug

        gate = (jnp.dot(mean_hi[:, kv * HD:(kv + 1) * HD], qs, preferred_element_type=F32)
                + jnp.dot(mean_lo[:, kv * HD:(kv + 1) * HD], qs, preferred_element_type=F32))
        gate = jnp.where(jj < b, gate, NEG)
        sel = jnp.zeros((nb, LANES), F32)
        for _ in range(TOPK):
            mx = jnp.max(gate, axis=0, keepdims=True)
            idx = jnp.min(jnp.where(gate == mx, jj, nb), axis=0, keepdims=True)
            hit = jj == jnp.where(mx > 0.5 * NEG, idx, -1)
            sel = jnp.where(hit, 1.0, sel)
            gate = jnp.where(hit, NEG, gate)
        sel_ref[kv] = sel

    s_own = [score_stage(0, b), score_stage(1, b)]
    p_own = []
    for kv in range(N_KV):
        p_own.append(own_softmax_stage(kv, s_own[kv]))
        if kv + 2 < N_KV:
            s_own.append(score_stage(kv + 2, b))
        else:
            s_ref[kv + 2 - N_KV] = score_stage(kv + 2 - N_KV, 0)
        if kv >= 1:
            value_stage(kv - 1, b, p_own[kv - 1])
    value_stage(N_KV - 1, b, p_own[N_KV - 1])
    p_ref[...] = jnp.zeros_like(p_ref)

    def past_block(j, carry):
        dist = ((b - j) * BLK).astype(F32)
        j_next = jnp.minimum(j + 1, b - 1)
        j_prev = jnp.maximum(j - 1, 0)
        p0 = past_softmax_stage(0, j, dist)
        value_stage(3, j_prev, p_ref[...])
        s_ref[2] = score_stage(2, j)
        p1 = past_softmax_stage(1, j, dist)
        value_stage(0, j, p0)
        s_ref[3] = score_stage(3, j)
        p2 = past_softmax_stage(2, j, dist)
        value_stage(1, j, p1)
        s_ref[0] = score_stage(0, j_next)
        p_ref[...] = past_softmax_stage(3, j, dist)
        value_stage(2, j, p2)
        s_ref[1] = score_stage(1, j_next)
        return carry

    lax.fori_loop(0, b, past_block, 0)
    value_stage(3, jnp.maximum(b - 1, 0), p_ref[...])
    for kv in range(N_KV):
        _store_heads(o_ref, kv, acc_ref[kv, :HD, :] / acc_ref[kv, HD:HD + 1, :])


def _moba_prompt_call(qT, katt, vT4, means, slope_rows):
    n, _, t = qT.shape
    nb = t // BLK
    whole = lambda i, b: (0, 0, 0)
    return pl.pallas_call(
        functools.partial(_moba_prompt_kernel, nb=nb),
        grid=(n, nb),
        in_specs=[pl.BlockSpec((1, QD, BLK), lambda i, b: (i, 0, b)),
                  pl.BlockSpec((1, nb, BLK, N_KV * KAUG), lambda i, b: (i, 0, 0, 0)),
                  pl.BlockSpec((1, nb, N_KV * VAUG, BLK), lambda i, b: (i, 0, 0, 0)),
                  pl.BlockSpec((1, nb, KVD), lambda i, b: (i, 0, 0)),
                  pl.BlockSpec(slope_rows.shape, whole)],
        out_specs=pl.BlockSpec((1, BLK, QD), lambda i, b: (i, b, 0)),
        out_shape=jax.ShapeDtypeStruct((n, t, QD), BF16),
        scratch_shapes=[pltpu.VMEM((N_KV, KAUG, LANES), BF16),
                        pltpu.VMEM((N_KV, nb, LANES), F32),
                        pltpu.VMEM((N_KV, 1, LANES), F32),
                        pltpu.VMEM((N_KV, VAUG, LANES), F32),
                        pltpu.VMEM((N_KV, BLK, LANES), F32),
                        pltpu.VMEM((BLK, LANES), BF16)],
        compiler_params=_params("arbitrary", "arbitrary"),
        name="moba_prompt",
    )(qT, katt, vT4, means, slope_rows)


def _prompt_rows(per_head):
    return jnp.repeat(per_head.astype(F32), BLK, axis=1).reshape(N_KV, 1, LANES)


def _new_token_scores(qb, knew, slope, tt, n_new):
    qf = qb.astype(F32)
    out = []
    for t in range(n_new):
        s = jnp.sum(qf * knew[t:t + 1, :], axis=1, keepdims=True)
        d = tt - float(t)
        out.append(jnp.where(d >= 0.0, s - slope * d, NEG))
    return out


def _fold_heads(o):
    rows = o.shape[0]
    step = rows // N_KV
    r = lax.broadcasted_iota(jnp.int32, (rows, KVD), 0) // step
    c = lax.broadcasted_iota(jnp.int32, (rows, KVD), 1) // HD
    o = jnp.where(r == c, o, 0.0)
    return o[0:step] + o[step:2 * step] + o[2 * step:3 * step] + o[3 * step:4 * step]


_NT = (((1,), (1,)), ((), ()))


def _swa_sample_kernel(qbd_ref, kc_ref, vc_ref, kn_ref, vn_ref, info_ref, o_ref, *, n_new):
    slope = info_ref[:, 0:1]
    tt = info_ref[:, 1:2]
    sink = info_ref[:, 2:3]
    wc = kc_ref.shape[2]
    cidx = lax.broadcasted_iota(jnp.int32, (1, wc), 1).astype(F32)
    delta = (float(wc) + tt) - cidx
    seqs = range(qbd_ref.shape[0])
    scores = [jnp.dot(qbd_ref[s_i], kc_ref[s_i].astype(BF16), preferred_element_type=F32)
              for s_i in seqs]
    probs, tails = [], []
    for s_i in seqs:
        s = jnp.where(delta < float(WINDOW), scores[s_i] - slope * delta, NEG)
        s_new = _new_token_scores(qbd_ref[s_i], kn_ref[s_i], slope, tt, n_new)
        m = jnp.maximum(jnp.max(s, axis=1, keepdims=True), sink)
        for sn in s_new:
            m = jnp.maximum(m, sn)
        p = jnp.exp2(s - m)
        denom = jnp.sum(p, axis=1, keepdims=True) + jnp.exp2(sink - m)
        o_new = jnp.zeros((qbd_ref.shape[1], KVD), F32)
        for t, sn in enumerate(s_new):
            e = jnp.exp2(sn - m)
            denom = denom + e
            o_new = o_new + e * vn_ref[s_i][t:t + 1, :]
        probs.append(p.astype(BF16))
        tails.append((o_new, denom))
    for s_i in seqs:
        o = lax.dot_general(probs[s_i], vc_ref[s_i].astype(BF16), _NT,
                            preferred_element_type=F32)
        o_new, denom = tails[s_i]
        o_ref[s_i] = _fold_heads((o + o_new) / denom)


def _swa_sample_call(qbd, kcT, vcT, kn, vn, info):
    nseq, rows, _ = qbd.shape
    wc = kcT.shape[2]
    n_new = kn.shape[1]
    sb = SWA_SEQS_PER_STEP
    assert nseq % sb == 0
    return pl.pallas_call(
        functools.partial(_swa_sample_kernel, n_new=n_new),
        grid=(nseq // sb,),
        in_specs=[pl.BlockSpec((sb, rows, KVD), lambda i: (i, 0, 0)),
                  pl.BlockSpec((sb, KVD, wc), lambda i: (i, 0, 0)),
                  pl.BlockSpec((sb, KVD, wc), lambda i: (i, 0, 0)),
                  pl.BlockSpec((sb, n_new, KVD), lambda i: (i, 0, 0)),
                  pl.BlockSpec((sb, n_new, KVD), lambda i: (i, 0, 0)),
                  pl.BlockSpec(info.shape, lambda i: (0, 0))],
        out_specs=pl.BlockSpec((sb, rows // N_KV, KVD), lambda i: (i, 0, 0)),
        out_shape=jax.ShapeDtypeStruct((nseq, rows // N_KV, KVD), F32),
        compiler_params=_params("arbitrary"),
        name="swa_sample",
    )(qbd, kcT, vcT, kn, vn, info)


def _moba_sample_kernel(pt_ref, kpool_ref, vpool_ref, qbd_ref, kn_ref, vn_ref, info_ref, o_ref,
                        kbuf_ref, vbuf_ref, ksem, vsem, m_ref, l_ref, oall_ref, meanT_ref,
                        *, layer, n_new, n_blk, past_len):
    pps = MOBA_PAGES_PER_STEP
    seq = pl.program_id(0)
    step = pl.program_id(1)
    steps = pl.num_programs(1)
    unit = seq * steps + step
    n_units = pl.num_programs(0) * steps
    slot = unit % 2

    def page_copies(page, page_slot, i):
        return (pltpu.make_async_copy(kpool_ref.at[layer, page], kbuf_ref.at[page_slot, i],
                                      ksem.at[page_slot]),
                pltpu.make_async_copy(vpool_ref.at[layer, page], vbuf_ref.at[page_slot, i],
                                      vsem.at[page_slot]))

    def start_unit(u, page_slot):
        for i in range(pps):
            page = pt_ref[u // steps, (u % steps) * pps + i]
            for copy in page_copies(page, page_slot, i):
                copy.start(priority=i % 2)

    @pl.when(unit == 0)
    def _():
        start_unit(unit, slot)

    @pl.when(unit + 1 < n_units)
    def _():
        start_unit(unit + 1, 1 - slot)

    for i in range(pps):
        for copy in page_copies(0, slot, i):
            copy.wait()
    k_refs = [kbuf_ref.at[slot, i] for i in range(pps)]
    v_refs = [vbuf_ref.at[slot, i] for i in range(pps)]

    slope = info_ref[:, 0:1]
    tt = info_ref[:, 1:2]
    qb = qbd_ref[0]
    rows = qb.shape[0]
    lane = lax.broadcasted_iota(jnp.int32, (rows, 128), 1)
    mlane = lax.broadcasted_iota(jnp.int32, (KVD, 128), 1)
    cidx = lax.broadcasted_iota(jnp.int32, (1, BLK), 1).astype(F32)
    pen0 = slope * (tt - cidx)

    @pl.when(step == 0)
    def _():
        m_ref[...] = jnp.zeros_like(m_ref)
        l_ref[...] = jnp.zeros_like(l_ref)
        meanT_ref[...] = jnp.zeros_like(meanT_ref)

    ppb = BLK // PAGE
    blocks = range(pps // ppb)
    scores = []
    for jb in blocks:
        kp = [k_refs[jb * ppb + i][...] for i in range(ppb)]
        scores.append(jnp.dot(qb, jnp.concatenate(kp, axis=1).astype(BF16),
                              preferred_element_type=F32))
    means, m_new, l_new = meanT_ref[...], m_ref[...], l_ref[...]
    probs = []
    for jb in blocks:
        j = step * (pps // ppb) + jb
        ksum = k_refs[jb * ppb][...]
        for i in range(1, ppb):
            ksum = ksum + k_refs[jb * ppb + i][...]
        kmean = jnp.sum(ksum, axis=1, keepdims=True) * (1.0 / BLK)
        means = jnp.where(mlane == j, kmean, means)
        s = scores[jb] - pen0
        mj = jnp.max(s, axis=1, keepdims=True)
        p = jnp.exp2(s - mj)
        m_new = jnp.where(lane == j, mj, m_new)
        l_new = jnp.where(lane == j, jnp.sum(p, axis=1, keepdims=True), l_new)
        probs.append(p.astype(BF16))
    meanT_ref[...], m_ref[...], l_ref[...] = means, m_new, l_new
    for jb in blocks:
        j = step * (pps // ppb) + jb
        vp = [v_refs[jb * ppb + i][...] for i in range(ppb)]
        oall_ref[j] = lax.dot_general(probs[jb], jnp.concatenate(vp, axis=1).astype(BF16), _NT,
                                      preferred_element_type=F32)

    @pl.when(step == pl.num_programs(1) - 1)
    def _():
        kn = kn_ref[0]
        vn = vn_ref[0]
        mean_hi, mean_lo = _split_bf16(meanT_ref[...])
        gate = (jnp.dot(qb, mean_hi, preferred_element_type=F32)
                + jnp.dot(qb, mean_lo, preferred_element_type=F32))[:, :n_blk]
        bl = lax.broadcasted_iota(jnp.int32, (rows, n_blk), 1)
        sel = jnp.zeros((rows, n_blk), jnp.bool_)
        for _ in range(min(TOPK, n_blk)):
            mx = jnp.max(gate, axis=1, keepdims=True)
            idx = jnp.min(jnp.where(gate == mx, bl, n_blk), axis=1, keepdims=True)
            hit = bl == idx
            sel = sel | hit
            gate = jnp.where(hit, NEG, gate)
        off = slope * (float(past_len) - bl.astype(F32) * float(BLK))
        mt = jnp.where(sel, m_ref[:, :n_blk] - off, NEG)
        s_new = _new_token_scores(qb, kn, slope, tt, n_new)
        m = jnp.max(mt, axis=1, keepdims=True)
        for sn in s_new:
            m = jnp.maximum(m, sn)
        w = jnp.where(sel, jnp.exp2(mt - m), 0.0)
        denom = jnp.sum(w * l_ref[:, :n_blk], axis=1, keepdims=True)
        o = jnp.zeros((rows, KVD), F32)
        for t, sn in enumerate(s_new):
            e = jnp.exp2(sn - m)
            denom = denom + e
            o = o + e * vn[t:t + 1, :]
        for j in range(n_blk):
            wj = jnp.sum(jnp.where(bl == j, w, 0.0), axis=1, keepdims=True)
            o = o + wj * oall_ref[j]
        o_ref[0] = _fold_heads(o / denom)


def _moba_sample_call(page_table, poolT_k, poolT_v, layer, qbd, kn, vn, info):
    nseq, rows, _ = qbd.shape
    n_pages = page_table.shape[1]
    n_new = kn.shape[1]
    pps = MOBA_PAGES_PER_STEP
    assert n_pages % pps == 0 and pps % (BLK // PAGE) == 0
    n_blk = n_pages * PAGE // BLK
    past_len = n_pages * PAGE
    assert past_len % BLK == 0 and n_new <= BLK and n_blk <= 128

    grid_spec = pltpu.PrefetchScalarGridSpec(
        num_scalar_prefetch=1,
        grid=(nseq, n_pages // pps),
        in_specs=[pl.BlockSpec(memory_space=pl.ANY),
                  pl.BlockSpec(memory_space=pl.ANY),
                  pl.BlockSpec((1, rows, KVD), lambda s, g, pt: (s, 0, 0)),
                  pl.BlockSpec((1, n_new, KVD), lambda s, g, pt: (s, 0, 0)),
                  pl.BlockSpec((1, n_new, KVD), lambda s, g, pt: (s, 0, 0)),
                  pl.BlockSpec(info.shape, lambda s, g, pt: (0, 0))],
        out_specs=pl.BlockSpec((1, rows // N_KV, KVD), lambda s, g, pt: (s, 0, 0)),
        scratch_shapes=[pltpu.VMEM((2, pps, KVD, PAGE), F32),
                        pltpu.VMEM((2, pps, KVD, PAGE), F32),
                        pltpu.SemaphoreType.DMA((2,)),
                        pltpu.SemaphoreType.DMA((2,)),
                        pltpu.VMEM((rows, 128), F32),
                        pltpu.VMEM((rows, 128), F32),
                        pltpu.VMEM((n_blk, rows, KVD), F32),
                        pltpu.VMEM((KVD, 128), F32)])
    return pl.pallas_call(
        functools.partial(_moba_sample_kernel, layer=layer, n_new=n_new, n_blk=n_blk,
                          past_len=past_len),
        grid_spec=grid_spec,
        out_shape=jax.ShapeDtypeStruct((nseq, rows // N_KV, KVD), F32),
        compiler_params=_params("arbitrary", "arbitrary"),
        name="moba_sample",
    )(page_table, poolT_k, poolT_v, qbd, kn, vn, info)


def _block_diag_queries(q, nseq, n_new):
    qr = q.reshape(nseq, n_new, N_KV, GROUP, HD).transpose(0, 2, 1, 3, 4)
    eye = jnp.eye(N_KV, dtype=q.dtype)
    qbd = qr[:, :, :, :, None, :] * eye[None, :, None, None, :, None]
    return qbd.reshape(nseq, N_KV * n_new * GROUP, KVD)


def _unfold_sample_out(o, nseq, n_new):
    o = o.reshape(nseq, n_new, GROUP, N_KV, HD).transpose(0, 1, 3, 2, 4)
    return o.reshape(nseq * n_new, QD).astype(BF16)


def _rows_major_view(cache):
    lead = cache.shape[:-3]
    nd = len(lead)
    perm = tuple(range(nd)) + (nd + 1, nd + 2, nd)
    return cache.transpose(perm).reshape(lead + (KVD, cache.shape[-3]))


def _sample_info(slopes_l2, sink_l2, n_new):
    def per_row(per_head):
        return jnp.broadcast_to(per_head.astype(F32)[:, None, :], (N_KV, n_new, GROUP)).reshape(-1)
    tt = jnp.broadcast_to(jnp.arange(n_new, dtype=F32)[None, :, None],
                          (N_KV, n_new, GROUP)).reshape(-1)
    return jnp.stack([per_row(slopes_l2), tt, per_row(sink_l2), jnp.zeros_like(tt)], axis=1)


def _swa_sample_attn(qs, ks, vs, cache_k, cache_v, slopes_l2, sink_l2):
    nseq = cache_k.shape[0]
    n_new = qs.shape[0] // nseq
    o = _swa_sample_call(_block_diag_queries(qs, nseq, n_new),
                         _rows_major_view(cache_k), _rows_major_view(cache_v),
                         ks.reshape(nseq, n_new, KVD), vs.reshape(nseq, n_new, KVD),
                         _sample_info(slopes_l2, sink_l2, n_new))
    return _unfold_sample_out(o, nseq, n_new)


def _moba_sample_attn(qs, ks, vs, poolT_k, poolT_v, layer, page_table, slopes_l2):
    nseq = page_table.shape[0]
    n_new = qs.shape[0] // nseq
    o = _moba_sample_call(page_table, poolT_k, poolT_v, layer,
                          _block_diag_queries(qs, nseq, n_new),
                          ks.reshape(nseq, n_new, KVD), vs.reshape(nseq, n_new, KVD),
                          _sample_info(slopes_l2, jnp.zeros_like(slopes_l2), n_new))
    return _unfold_sample_out(o, nseq, n_new)


def kernel(x_prompt, x_sample, cache_swa_k, cache_swa_v, cache_moba_k, cache_moba_v, page_table,
           g_attn, w_qkv, g_q, g_k, sinks, w_o, g_mlp, w_up, w_down):
    n, t, _ = x_prompt.shape
    nseq, n_new, _ = x_sample.shape
    depth = w_qkv.shape[0]
    nb = t // BLK
    assert t % BLK == 0

    head = jnp.arange(1, N_HEADS + 1, dtype=F32)
    slopes_l2 = jnp.exp2(-8.0 * head / N_HEADS).reshape(N_KV, GROUP) * LOG2E

    wqkv_b = w_qkv.astype(BF16)
    wo_b = w_o.astype(BF16)
    wup_b = w_up.astype(BF16)
    wdn_b = w_down.astype(BF16)
    poolT_k = _rows_major_view(cache_moba_k)
    poolT_v = _rows_major_view(cache_moba_v)

    ga = g_attn.reshape(depth, 1, D_MODEL)
    gm = g_mlp.reshape(depth, 1, D_MODEL)
    gq = jnp.tile(g_q, (1, N_KV)).reshape(depth, 1, KVD)
    gk = jnp.tile(g_k, (1, N_KV)).reshape(depth, 1, KVD)
    slope_rows = _prompt_rows(slopes_l2)

    def rows_first(xT):
        return xT.reshape(n, N_KV, HD, xT.shape[-1]).transpose(0, 3, 1, 2)

    xp = x_prompt.reshape(n * t, D_MODEL)
    xs = x_sample.reshape(nseq * n_new, D_MODEL)
    swa_kp, swa_vp, swa_ks, swa_vs = [], [], [], []
    moba_kp, moba_vp, moba_ks, moba_vs = [], [], [], []
    for i in range(depth):
        is_swa = i % 2 == 0
        j = i // 2
        qT, kT, vT, katt, vT4, means = _qkv_prompt_call(xp, ga, wqkv_b, gq, gk, i, n, t)
        qs, ks, vs = _qkv_sample_call(xs, ga, wqkv_b, gq, gk, i)
        ks5 = ks.reshape(nseq, n_new, N_KV, HD)
        vs5 = vs.reshape(nseq, n_new, N_KV, HD)
        if is_swa:
            sink_l2 = sinks[j].reshape(N_KV, GROUP).astype(F32) * LOG2E
            attn_p = _swa_prompt_call(qT, katt, vT4, slope_rows, _prompt_rows(sink_l2))
            attn_s = _swa_sample_attn(qs, ks, vs, cache_swa_k[j], cache_swa_v[j],
                                      slopes_l2, sink_l2)
            w = min(WINDOW, t)
            wc = cache_swa_k.shape[2]
            swa_kp.append(rows_first(kT[:, :, t - w:]))
            swa_vp.append(rows_first(vT[:, :, t - w:]))
            swa_ks.append(jnp.concatenate([cache_swa_k[j], ks5], axis=1)[:, -wc:])
            swa_vs.append(jnp.concatenate([cache_swa_v[j], vs5], axis=1)[:, -wc:])
        else:
            attn_p = _moba_prompt_call(qT, katt, vT4, means.reshape(n, nb, KVD), slope_rows)
            attn_s = _moba_sample_attn(qs, ks, vs, poolT_k, poolT_v, j, page_table, slopes_l2)
            moba_kp.append(rows_first(kT))
            moba_vp.append(rows_first(vT))
            moba_ks.append(ks5)
            moba_vs.append(vs5)
        xp = _mlp_call(xp, attn_p.reshape(n * t, QD), wo_b, gm, wup_b, wdn_b, i)
        xs = _mlp_call(xs, attn_s, wo_b, gm, wup_b, wdn_b, i)
    return (xp.reshape(n, t, D_MODEL), xs.reshape(nseq, n_new, D_MODEL),
            jnp.stack(swa_kp), jnp.stack(swa_vp), jnp.stack(swa_ks), jnp.stack(swa_vs),
            jnp.stack(moba_kp), jnp.stack(moba_vp), jnp.stack(moba_ks), jnp.stack(moba_vs))
```

```python
import functools
import math

import jax
import jax.numpy as jnp
from jax import lax
from jax.experimental import pallas as pl
from jax.experimental.pallas import tpu as pltpu

F32 = jnp.float32
BF16 = jnp.bfloat16

D_MODEL = 1024
N_HEADS = 16
N_KV = 4
GROUP = N_HEADS // N_KV
HD = D_MODEL // N_HEADS
QD = N_HEADS * HD
KVD = N_KV * HD
QKV_DIM = QD + 2 * KVD
D_FF = 4 * D_MODEL
WINDOW = 128
BLK = 256
TOPK = 3
PAGE = 128
EPS = 1e-6
NEG = -1e30
LOG2E = math.log2(math.e)

KAUG = 2 * HD
VAUG = HD + 16
N_SLOPE_PARTS = 3
LANES = GROUP * BLK

ROW_TILE = 512
FF_CHUNK = 1024
SWA_SEQS_PER_STEP = 8
MOBA_PAGES_PER_STEP = 16
MOBA_PAGE_SLOTS = 3
VMEM_LIMIT = 56 * 1024 * 1024


def _params(*sem):
    return pltpu.CompilerParams(dimension_semantics=sem, vmem_limit_bytes=VMEM_LIMIT)


def _split_bf16(x):
    hi = x.astype(BF16)
    lo = (x - hi.astype(F32)).astype(BF16)
    return hi, lo


def _project_qkv(x_ref, g_ref, w_ref, gq_ref, gk_ref):
    x = x_ref[...]
    ms = jnp.mean(x * x, axis=-1, keepdims=True)
    xn = (x * lax.rsqrt(ms + EPS)) * g_ref[...]
    qkv = jnp.dot(xn.astype(BF16), w_ref[...], preferred_element_type=F32)

    r = lax.broadcasted_iota(jnp.int32, (KVD, KVD), 0) // HD
    c = lax.broadcasted_iota(jnp.int32, (KVD, KVD), 1) // HD
    seg = jnp.where(r == c, 1.0, 0.0).astype(BF16)

    def head_norm(z, gain):
        hi, lo = _split_bf16(z * z)
        tot = (jnp.dot(hi, seg, preferred_element_type=F32)
               + jnp.dot(lo, seg, preferred_element_type=F32))
        return (z * lax.rsqrt(tot * (1.0 / HD) + EPS)) * gain

    q = [head_norm(qkv[:, c * KVD:(c + 1) * KVD], gq_ref[...]) * (HD ** -0.5 * LOG2E)
         for c in range(QD // KVD)]
    k = head_norm(qkv[:, QD:QD + KVD], gk_ref[...])
    return q, k, qkv[:, QD + KVD:]


def _qkv_sample_kernel(x_ref, g_ref, w_ref, gq_ref, gk_ref, q_ref, k_ref, v_ref):
    q, k, v = _project_qkv(x_ref, g_ref, w_ref, gq_ref, gk_ref)
    for c, qc in enumerate(q):
        q_ref[:, c * KVD:(c + 1) * KVD] = qc.astype(q_ref.dtype)
    k_ref[...] = k
    v_ref[...] = v


def _qkv_prompt_kernel(x_ref, g_ref, w_ref, gq_ref, gk_ref,
                       qT_ref, kT_ref, vT_ref, katt_ref, vT4_ref, mean_ref):
    q, k, v = _project_qkv(x_ref, g_ref, w_ref, gq_ref, gk_ref)
    rows = k.shape[0]
    for c, qc in enumerate(q):
        qT_ref[0, c * KVD:(c + 1) * KVD, :] = qc.T.astype(qT_ref.dtype)
    kT_ref[0] = k.T
    vT = v.T
    vT_ref[0] = vT

    lane = lax.broadcasted_iota(jnp.int32, (rows, 2 * HD), 1)
    key_off = (lax.broadcasted_iota(jnp.int32, (rows, 2 * HD), 0) % BLK).astype(F32)
    aug = jnp.where((lane >= HD) & (lane < HD + N_SLOPE_PARTS), key_off, 0.0)
    groups = []
    for pair in range(N_KV // 2):
        two = k[:, pair * 2 * HD:(pair + 1) * 2 * HD]
        groups.append(jnp.where(lane < HD, two, aug))
        groups.append(jnp.where(lane < HD, pltpu.roll(two, HD, axis=1), aug))
    katt = jnp.concatenate(groups, axis=1).astype(katt_ref.dtype)
    ones_rows = jnp.where(lax.broadcasted_iota(jnp.int32, (VAUG - HD, BLK), 0) == 0, 1.0, 0.0)
    for b in range(rows // BLK):
        katt_ref[0, b] = katt[b * BLK:(b + 1) * BLK]
        vblk = vT[:, b * BLK:(b + 1) * BLK]
        vT4_ref[0, b] = jnp.concatenate(
            [piece for kv in range(N_KV) for piece in (vblk[kv * HD:(kv + 1) * HD], ones_rows)],
            axis=0).astype(vT4_ref.dtype)
        mean_ref[0, b:b + 1, :] = jnp.sum(
            k[b * BLK:(b + 1) * BLK], axis=0, keepdims=True) * (1.0 / BLK)


def _layer_spec(shape, layer):
    nd = len(shape)
    return pl.BlockSpec((None,) + tuple(shape), lambda *_: (layer,) + (0,) * nd,
                        pipeline_mode=pl.Buffered(1))


def _qkv_in_specs(x_spec, layer):
    return [x_spec,
            _layer_spec((1, D_MODEL), layer),
            _layer_spec((D_MODEL, QKV_DIM), layer),
            _layer_spec((1, KVD), layer),
            _layer_spec((1, KVD), layer)]


def _qkv_sample_call(x, g_attn, wqkv, gq, gk, layer):
    rows = x.shape[0]
    assert rows % ROW_TILE == 0
    return pl.pallas_call(
        _qkv_sample_kernel,
        grid=(rows // ROW_TILE,),
        in_specs=_qkv_in_specs(pl.BlockSpec((ROW_TILE, D_MODEL), lambda i: (i, 0)), layer),
        out_specs=[pl.BlockSpec((ROW_TILE, QD), lambda i: (i, 0)),
                   pl.BlockSpec((ROW_TILE, KVD), lambda i: (i, 0)),
                   pl.BlockSpec((ROW_TILE, KVD), lambda i: (i, 0))],
        out_shape=[jax.ShapeDtypeStruct((rows, QD), BF16),
                   jax.ShapeDtypeStruct((rows, KVD), F32),
                   jax.ShapeDtypeStruct((rows, KVD), F32)],
        compiler_params=_params("arbitrary"),
        name="qkv_sample",
    )(x, g_attn, wqkv, gq, gk)


def _qkv_prompt_call(x, g_attn, wqkv, gq, gk, layer, n, t):
    assert t % ROW_TILE == 0 and ROW_TILE % BLK == 0
    steps = t // ROW_TILE
    bpt = ROW_TILE // BLK
    nb = t // BLK
    return pl.pallas_call(
        _qkv_prompt_kernel,
        grid=(n, steps),
        in_specs=_qkv_in_specs(
            pl.BlockSpec((ROW_TILE, D_MODEL), lambda s, i: (s * steps + i, 0)), layer),
        out_specs=[pl.BlockSpec((1, QD, ROW_TILE), lambda s, i: (s, 0, i)),
                   pl.BlockSpec((1, KVD, ROW_TILE), lambda s, i: (s, 0, i)),
                   pl.BlockSpec((1, KVD, ROW_TILE), lambda s, i: (s, 0, i)),
                   pl.BlockSpec((1, bpt, BLK, N_KV * KAUG), lambda s, i: (s, i, 0, 0)),
                   pl.BlockSpec((1, bpt, N_KV * VAUG, BLK), lambda s, i: (s, i, 0, 0)),
                   pl.BlockSpec((1, bpt, KVD), lambda s, i: (s * steps + i, 0, 0))],
        out_shape=[jax.ShapeDtypeStruct((n, QD, t), BF16),
                   jax.ShapeDtypeStruct((n, KVD, t), F32),
                   jax.ShapeDtypeStruct((n, KVD, t), F32),
                   jax.ShapeDtypeStruct((n, nb, BLK, N_KV * KAUG), BF16),
                   jax.ShapeDtypeStruct((n, nb, N_KV * VAUG, BLK), BF16),
                   jax.ShapeDtypeStruct((n * steps, bpt, KVD), F32)],
        compiler_params=_params("arbitrary", "arbitrary"),
        name="qkv_prompt",
    )(x, g_attn, wqkv, gq, gk)


def _mlp_kernel(x_ref, a_ref, wo_ref, g_ref, wup_ref, wdn_ref, o_ref):
    x1 = x_ref[...] + jnp.dot(a_ref[...], wo_ref[...], preferred_element_type=F32)
    ms = jnp.mean(x1 * x1, axis=-1, keepdims=True)
    h = ((x1 * lax.rsqrt(ms + EPS)) * g_ref[...]).astype(BF16)
    acc = x1
    for c in range(D_FF // FF_CHUNK):
        u = jnp.dot(h, wup_ref[:, c * FF_CHUNK:(c + 1) * FF_CHUNK], preferred_element_type=F32)
        u = jnp.square(jnp.maximum(u, 0.0)).astype(BF16)
        acc = acc + jnp.dot(u, wdn_ref[c * FF_CHUNK:(c + 1) * FF_CHUNK, :],
                            preferred_element_type=F32)
    o_ref[...] = acc


def _mlp_call(x, attn, wo, g_mlp, wup, wdn, layer):
    rows = x.shape[0]
    assert rows % ROW_TILE == 0
    return pl.pallas_call(
        _mlp_kernel,
        grid=(rows // ROW_TILE,),
        in_specs=[pl.BlockSpec((ROW_TILE, D_MODEL), lambda i: (i, 0)),
                  pl.BlockSpec((ROW_TILE, QD), lambda i: (i, 0)),
                  _layer_spec((QD, D_MODEL), layer),
                  _layer_spec((1, D_MODEL), layer),
                  _layer_spec((D_MODEL, D_FF), layer),
                  _layer_spec((D_FF, D_MODEL), layer)],
        out_specs=pl.BlockSpec((ROW_TILE, D_MODEL), lambda i: (i, 0)),
        out_shape=jax.ShapeDtypeStruct((rows, D_MODEL), F32),
        compiler_params=_params("arbitrary"),
        name="oproj_mlp",
    )(x, attn, wo, g_mlp, wup, wdn)


def _aug_queries(qT_ref, slope, kv):
    qs = jnp.concatenate(
        [qT_ref[0, (kv * GROUP + g) * HD:(kv * GROUP + g + 1) * HD, :] for g in range(GROUP)],
        axis=1)
    row = lax.broadcasted_iota(jnp.int32, (KAUG - HD, LANES), 0)
    aug = jnp.zeros((KAUG - HD, LANES), F32)
    rest = slope
    for part in range(N_SLOPE_PARTS):
        piece = rest.astype(BF16).astype(F32)
        aug = jnp.where(row == part, piece, aug)
        rest = rest - piece
    return qs, jnp.concatenate([qs, aug.astype(BF16)], axis=0)


def _query_minus_key_offset():
    rr = lax.broadcasted_iota(jnp.int32, (1, LANES), 1) % BLK
    cc = lax.broadcasted_iota(jnp.int32, (BLK, 1), 0)
    return rr - cc


def _store_heads(o_ref, kv, o):
    for pair in range(GROUP // 2):
        two = jnp.concatenate([o[:, (2 * pair) * BLK:(2 * pair + 1) * BLK],
                               o[:, (2 * pair + 1) * BLK:(2 * pair + 2) * BLK]], axis=0)
        lane0 = (kv * GROUP + 2 * pair) * HD
        o_ref[0, :, lane0:lane0 + 2 * HD] = two.T.astype(o_ref.dtype)


def _swa_prompt_kernel(qT_ref, kc_ref, kp_ref, vc_ref, vp_ref, slope_ref, sink_ref, o_ref):
    b = pl.program_id(1)
    d = _query_minus_key_offset()
    cur_ok = (d >= 0) & (d < WINDOW)
    prev_ok = d < WINDOW - BLK
    prev_bias = jnp.where(b > 0, 0.0, NEG)
    rr = (lax.broadcasted_iota(jnp.int32, (1, LANES), 1) % BLK).astype(F32)
    scores = []
    for kv in range(N_KV):
        _, qaug = _aug_queries(qT_ref, slope_ref[kv], kv)
        scores.append((jnp.dot(kc_ref[0, 0, :, kv * KAUG:(kv + 1) * KAUG], qaug,
                               preferred_element_type=F32),
                       jnp.dot(kp_ref[0, 0, :, kv * KAUG:(kv + 1) * KAUG], qaug,
                               preferred_element_type=F32)))
    for kv in range(N_KV):
        slope = slope_ref[kv]
        sink = sink_ref[kv] + slope * rr
        prev_off = prev_bias - slope * float(BLK)
        s1 = jnp.where(cur_ok, scores[kv][0], NEG)
        s2 = jnp.where(prev_ok, scores[kv][1], NEG)
        m = jnp.maximum(jnp.maximum(jnp.max(s1, axis=0, keepdims=True),
                                    jnp.max(s2, axis=0, keepdims=True) + prev_off), sink)
        p1 = jnp.exp2(s1 - m).astype(BF16)
        p2 = jnp.exp2(s2 + (prev_off - m)).astype(BF16)
        acc = (jnp.dot(vc_ref[0, 0, kv * VAUG:(kv + 1) * VAUG, :], p1,
                       preferred_element_type=F32)
               + jnp.dot(vp_ref[0, 0, kv * VAUG:(kv + 1) * VAUG, :], p2,
                         preferred_element_type=F32))
        denom = acc[HD:HD + 1, :] + jnp.exp2(sink - m)
        _store_heads(o_ref, kv, acc[:HD, :] / denom)


def _swa_prompt_call(qT, katt, vT4, slope_rows, sink_rows):
    n, _, t = qT.shape
    nb = t // BLK
    cur = lambda i, b: (i, b, 0, 0)
    prev = lambda i, b: (i, jnp.maximum(b - 1, 0), 0, 0)
    whole = lambda i, b: (0, 0, 0)
    return pl.pallas_call(
        _swa_prompt_kernel,
        grid=(n, nb),
        in_specs=[pl.BlockSpec((1, QD, BLK), lambda i, b: (i, 0, b)),
                  pl.BlockSpec((1, 1, BLK, N_KV * KAUG), cur),
                  pl.BlockSpec((1, 1, BLK, N_KV * KAUG), prev),
                  pl.BlockSpec((1, 1, N_KV * VAUG, BLK), cur),
                  pl.BlockSpec((1, 1, N_KV * VAUG, BLK), prev),
                  pl.BlockSpec(slope_rows.shape, whole),
                  pl.BlockSpec(sink_rows.shape, whole)],
        out_specs=pl.BlockSpec((1, BLK, QD), lambda i, b: (i, b, 0)),
        out_shape=jax.ShapeDtypeStruct((n, t, QD), BF16),
        compiler_params=_params("arbitrary", "arbitrary"),
        name="swa_prompt",
    )(qT, katt, katt, vT4, vT4, slope_rows, sink_rows)


def _moba_prompt_kernel(qT_ref, k_ref, vT_ref, mean_ref, slope_ref, o_ref,
                        qaug_ref, sel_ref, m_ref, acc_ref, s_ref, p_ref, *, nb):
    assert N_KV == 4
    b = pl.program_id(1)
    causal = _query_minus_key_offset() >= 0
    jj = lax.broadcasted_iota(jnp.int32, (nb, LANES), 0)
    mean_hi, mean_lo = _split_bf16(mean_ref[0])

    def score_stage(kv, j):
        return jnp.dot(k_ref[0, j, :, kv * KAUG:(kv + 1) * KAUG], qaug_ref[kv],
                       preferred_element_type=F32)

    def value_stage(kv, j, p):
        acc_ref[kv] += jnp.dot(vT_ref[0, j, kv * VAUG:(kv + 1) * VAUG, :], p,
                               preferred_element_type=F32)

    def own_softmax_stage(kv, s):
        s = jnp.where(causal, s, NEG)
        m = jnp.max(s, axis=0, keepdims=True)
        m_ref[kv] = m
        acc_ref[kv] = jnp.zeros((VAUG, LANES), F32)
        return jnp.exp2(s - m).astype(BF16)

    def past_softmax_stage(kv, j, dist):
        s = s_ref[kv]
        bias = jnp.where(sel_ref[kv, pl.ds(j, 1), :] > 0.5, -dist * slope_ref[kv], NEG)
        m_old = m_ref[kv]
        m_new = jnp.maximum(m_old, jnp.max(s, axis=0, keepdims=True) + bias)
        acc_ref[kv] = jnp.exp2(m_old - m_new) * acc_ref[kv]
        m_ref[kv] = m_new
        return jnp.exp2(s + (bias - m_new)).astype(BF16)

    for kv in range(N_KV):
        qs, qaug = _aug_queries(qT_ref, slope_ref[kv], kv)
        qaug_ref[kv] = qaug

        gate = (jnp.dot(mean_hi[:, kv * HD:(kv + 1) * HD], qs, preferred_element_type=F32)
                + jnp.dot(mean_lo[:, kv * HD:(kv + 1) * HD], qs, preferred_element_type=F32))
        gate = jnp.where(jj < b, gate, NEG)
        sel = jnp.zeros((nb, LANES), F32)
        for _ in range(TOPK):
            mx = jnp.max(gate, axis=0, keepdims=True)
            idx = jnp.min(jnp.where(gate == mx, jj, nb), axis=0, keepdims=True)
            hit = jj == jnp.where(mx > 0.5 * NEG, idx, -1)
            sel = jnp.where(hit, 1.0, sel)
            gate = jnp.where(hit, NEG, gate)
        sel_ref[kv] = sel

    s_own = [score_stage(0, b), score_stage(1, b)]
    p_own = []
    for kv in range(N_KV):
        p_own.append(own_softmax_stage(kv, s_own[kv]))
        if kv + 2 < N_KV:
            s_own.append(score_stage(kv + 2, b))
        else:
            s_ref[kv + 2 - N_KV] = score_stage(kv + 2 - N_KV, 0)
        if kv >= 1:
            value_stage(kv - 1, b, p_own[kv - 1])
    value_stage(N_KV - 1, b, p_own[N_KV - 1])
    p_ref[...] = jnp.zeros_like(p_ref)

    def past_block(j, carry):
        dist = ((b - j) * BLK).astype(F32)
        j_next = jnp.minimum(j + 1, b - 1)
        j_prev = jnp.maximum(j - 1, 0)
        p0 = past_softmax_stage(0, j, dist)
        value_stage(3, j_prev, p_ref[...])
        s_ref[2] = score_stage(2, j)
        p1 = past_softmax_stage(1, j, dist)
        value_stage(0, j, p0)
        s_ref[3] = score_stage(3, j)
        p2 = past_softmax_stage(2, j, dist)
        value_stage(1, j, p1)
        s_ref[0] = score_stage(0, j_next)
        p_ref[...] = past_softmax_stage(3, j, dist)
        value_stage(2, j, p2)
        s_ref[1] = score_stage(1, j_next)
        return carry

    def two_past_blocks(i, carry):
        return past_block(2 * i + 1, past_block(2 * i, carry))

    lax.fori_loop(0, b // 2, two_past_blocks, 0)
    lax.fori_loop(b // 2 * 2, b, past_block, 0)
    value_stage(3, jnp.maximum(b - 1, 0), p_ref[...])
    for kv in range(N_KV):
        _store_heads(o_ref, kv, acc_ref[kv, :HD, :] / acc_ref[kv, HD:HD + 1, :])


def _moba_prompt_call(qT, katt, vT4, means, slope_rows):
    n, _, t = qT.shape
    nb = t // BLK
    whole = lambda i, b: (0, 0, 0)
    return pl.pallas_call(
        functools.partial(_moba_prompt_kernel, nb=nb),
        grid=(n, nb),
        in_specs=[pl.BlockSpec((1, QD, BLK), lambda i, b: (i, 0, b)),
                  pl.BlockSpec((1, nb, BLK, N_KV * KAUG), lambda i, b: (i, 0, 0, 0)),
                  pl.BlockSpec((1, nb, N_KV * VAUG, BLK), lambda i, b: (i, 0, 0, 0)),
                  pl.BlockSpec((1, nb, KVD), lambda i, b: (i, 0, 0)),
                  pl.BlockSpec(slope_rows.shape, whole)],
        out_specs=pl.BlockSpec((1, BLK, QD), lambda i, b: (i, b, 0)),
        out_shape=jax.ShapeDtypeStruct((n, t, QD), BF16),
        scratch_shapes=[pltpu.VMEM((N_KV, KAUG, LANES), BF16),
                        pltpu.VMEM((N_KV, nb, LANES), F32),
                        pltpu.VMEM((N_KV, 1, LANES), F32),
                        pltpu.VMEM((N_KV, VAUG, LANES), F32),
                        pltpu.VMEM((N_KV, BLK, LANES), F32),
                        pltpu.VMEM((BLK, LANES), BF16)],
        compiler_params=_params("arbitrary", "arbitrary"),
        name="moba_prompt",
    )(qT, katt, vT4, means, slope_rows)


def _prompt_rows(per_head):
    return jnp.repeat(per_head.astype(F32), BLK, axis=1).reshape(N_KV, 1, LANES)


def _new_token_scores(qb, knew, slope, tt, n_new):
    qf = qb.astype(F32)
    out = []
    for t in range(n_new):
        s = jnp.sum(qf * knew[t:t + 1, :], axis=1, keepdims=True)
        d = tt - float(t)
        out.append(jnp.where(d >= 0.0, s - slope * d, NEG))
    return out


def _fold_heads(o):
    rows = o.shape[0]
    step = rows // N_KV
    r = lax.broadcasted_iota(jnp.int32, (rows, KVD), 0) // step
    c = lax.broadcasted_iota(jnp.int32, (rows, KVD), 1) // HD
    o = jnp.where(r == c, o, 0.0)
    return o[0:step] + o[step:2 * step] + o[2 * step:3 * step] + o[3 * step:4 * step]


_NT = (((1,), (1,)), ((), ()))


def _swa_sample_kernel(qbd_ref, kc_ref, vc_ref, kn_ref, vn_ref, info_ref, o_ref, *, n_new):
    slope = info_ref[:, 0:1]
    tt = info_ref[:, 1:2]
    sink = info_ref[:, 2:3]
    wc = kc_ref.shape[2]
    cidx = lax.broadcasted_iota(jnp.int32, (1, wc), 1).astype(F32)
    delta = (float(wc) + tt) - cidx
    seqs = range(qbd_ref.shape[0])
    scores = [jnp.dot(qbd_ref[s_i], kc_ref[s_i].astype(BF16), preferred_element_type=F32)
              for s_i in seqs]
    probs, tails = [], []
    for s_i in seqs:
        s = jnp.where(delta < float(WINDOW), scores[s_i] - slope * delta, NEG)
        s_new = _new_token_scores(qbd_ref[s_i], kn_ref[s_i], slope, tt, n_new)
        m = jnp.maximum(jnp.max(s, axis=1, keepdims=True), sink)
        for sn in s_new:
            m = jnp.maximum(m, sn)
        p = jnp.exp2(s - m)
        denom = jnp.sum(p, axis=1, keepdims=True) + jnp.exp2(sink - m)
        o_new = jnp.zeros((qbd_ref.shape[1], KVD), F32)
        for t, sn in enumerate(s_new):
            e = jnp.exp2(sn - m)
            denom = denom + e
            o_new = o_new + e * vn_ref[s_i][t:t + 1, :]
        probs.append(p.astype(BF16))
        tails.append((o_new, denom))
    for s_i in seqs:
        o = lax.dot_general(probs[s_i], vc_ref[s_i].astype(BF16), _NT,
                            preferred_element_type=F32)
        o_new, denom = tails[s_i]
        o_ref[s_i] = _fold_heads((o + o_new) / denom)


def _swa_sample_call(qbd, kcT, vcT, kn, vn, info):
    nseq, rows, _ = qbd.shape
    wc = kcT.shape[2]
    n_new = kn.shape[1]
    sb = SWA_SEQS_PER_STEP
    assert nseq % sb == 0
    return pl.pallas_call(
        functools.partial(_swa_sample_kernel, n_new=n_new),
        grid=(nseq // sb,),
        in_specs=[pl.BlockSpec((sb, rows, KVD), lambda i: (i, 0, 0)),
                  pl.BlockSpec((sb, KVD, wc), lambda i: (i, 0, 0)),
                  pl.BlockSpec((sb, KVD, wc), lambda i: (i, 0, 0)),
                  pl.BlockSpec((sb, n_new, KVD), lambda i: (i, 0, 0)),
                  pl.BlockSpec((sb, n_new, KVD), lambda i: (i, 0, 0)),
                  pl.BlockSpec(info.shape, lambda i: (0, 0))],
        out_specs=pl.BlockSpec((sb, rows // N_KV, KVD), lambda i: (i, 0, 0)),
        out_shape=jax.ShapeDtypeStruct((nseq, rows // N_KV, KVD), F32),
        compiler_params=_params("arbitrary"),
        name="swa_sample",
    )(qbd, kcT, vcT, kn, vn, info)


def _moba_sample_kernel(pt_ref, kpool_ref, vpool_ref, qbd_ref, kn_ref, vn_ref, info_ref, o_ref,
                        kbuf_ref, vbuf_ref, ksem, vsem, m_ref, l_ref, oall_ref, meanT_ref,
                        *, layer, n_new, n_blk, past_len):
    pps = MOBA_PAGES_PER_STEP
    seq = pl.program_id(0)
    step = pl.program_id(1)
    steps = pl.num_programs(1)
    unit = seq * steps + step
    n_units = pl.num_programs(0) * steps
    slot = unit % MOBA_PAGE_SLOTS

    def page_copies(page, page_slot, i):
        return (pltpu.make_async_copy(kpool_ref.at[layer, page], kbuf_ref.at[page_slot, i],
                                      ksem.at[page_slot]),
                pltpu.make_async_copy(vpool_ref.at[layer, page], vbuf_ref.at[page_slot, i],
                                      vsem.at[page_slot]))

    def start_unit(u, page_slot):
        for i in range(pps):
            page = pt_ref[u // steps, (u % steps) * pps + i]
            for copy in page_copies(page, page_slot, i):
                copy.start(priority=i % 2)

    ahead = MOBA_PAGE_SLOTS - 1

    @pl.when(unit == 0)
    def _():
        for u in range(ahead):
            start_unit(u, u)

    @pl.when(unit + ahead < n_units)
    def _():
        start_unit(unit + ahead, (unit + ahead) % MOBA_PAGE_SLOTS)

    for i in range(pps):
        for copy in page_copies(0, slot, i):
            copy.wait()
    k_refs = [kbuf_ref.at[slot, i] for i in range(pps)]
    v_refs = [vbuf_ref.at[slot, i] for i in range(pps)]

    slope = info_ref[:, 0:1]
    tt = info_ref[:, 1:2]
    qb = qbd_ref[0]
    rows = qb.shape[0]
    lane = lax.broadcasted_iota(jnp.int32, (rows, 128), 1)
    mlane = lax.broadcasted_iota(jnp.int32, (KVD, 128), 1)
    cidx = lax.broadcasted_iota(jnp.int32, (1, BLK), 1).astype(F32)
    pen0 = slope * (tt - cidx)

    @pl.when(step == 0)
    def _():
        m_ref[...] = jnp.zeros_like(m_ref)
        l_ref[...] = jnp.zeros_like(l_ref)
        meanT_ref[...] = jnp.zeros_like(meanT_ref)

    ppb = BLK // PAGE
    blocks = range(pps // ppb)
    scores = []
    for jb in blocks:
        kp = [k_refs[jb * ppb + i][...] for i in range(ppb)]
        scores.append(jnp.dot(qb, jnp.concatenate(kp, axis=1).astype(BF16),
                              preferred_element_type=F32))
    means, m_new, l_new = meanT_ref[...], m_ref[...], l_ref[...]
    probs = []
    for jb in blocks:
        j = step * (pps // ppb) + jb
        ksum = k_refs[jb * ppb][...]
        for i in range(1, ppb):
            ksum = ksum + k_refs[jb * ppb + i][...]
        kmean = jnp.sum(ksum, axis=1, keepdims=True) * (1.0 / BLK)
        means = jnp.where(mlane == j, kmean, means)
        s = scores[jb] - pen0
        mj = jnp.max(s, axis=1, keepdims=True)
        p = jnp.exp2(s - mj)
        m_new = jnp.where(lane == j, mj, m_new)
        l_new = jnp.where(lane == j, jnp.sum(p, axis=1, keepdims=True), l_new)
        probs.append(p.astype(BF16))
    meanT_ref[...], m_ref[...], l_ref[...] = means, m_new, l_new
    for jb in blocks:
        j = step * (pps // ppb) + jb
        vp = [v_refs[jb * ppb + i][...] for i in range(ppb)]
        oall_ref[j] = lax.dot_general(probs[jb], jnp.concatenate(vp, axis=1).astype(BF16), _NT,
                                      preferred_element_type=F32)

    @pl.when(step == pl.num_programs(1) - 1)
    def _():
        kn = kn_ref[0]
        vn = vn_ref[0]
        mean_hi, mean_lo = _split_bf16(meanT_ref[...])
        gate = (jnp.dot(qb, mean_hi, preferred_element_type=F32)
                + jnp.dot(qb, mean_lo, preferred_element_type=F32))[:, :n_blk]
        bl = lax.broadcasted_iota(jnp.int32, (rows, n_blk), 1)
        sel = jnp.zeros((rows, n_blk), jnp.bool_)
        for _ in range(min(TOPK, n_blk)):
            mx = jnp.max(gate, axis=1, keepdims=True)
            idx = jnp.min(jnp.where(gate == mx, bl, n_blk), axis=1, keepdims=True)
            hit = bl == idx
            sel = sel | hit
            gate = jnp.where(hit, NEG, gate)
        off = slope * (float(past_len) - bl.astype(F32) * float(BLK))
        mt = jnp.where(sel, m_ref[:, :n_blk] - off, NEG)
        s_new = _new_token_scores(qb, kn, slope, tt, n_new)
        m = jnp.max(mt, axis=1, keepdims=True)
        for sn in s_new:
            m = jnp.maximum(m, sn)
        w = jnp.where(sel, jnp.exp2(mt - m), 0.0)
        denom = jnp.sum(w * l_ref[:, :n_blk], axis=1, keepdims=True)
        o = jnp.zeros((rows, KVD), F32)
        for t, sn in enumerate(s_new):
            e = jnp.exp2(sn - m)
            denom = denom + e
            o = o + e * vn[t:t + 1, :]
        for j in range(n_blk):
            wj = jnp.sum(jnp.where(bl == j, w, 0.0), axis=1, keepdims=True)
            o = o + wj * oall_ref[j]
        o_ref[0] = _fold_heads(o / denom)


def _moba_sample_call(page_table, poolT_k, poolT_v, layer, qbd, kn, vn, info):
    nseq, rows, _ = qbd.shape
    n_pages = page_table.shape[1]
    n_new = kn.shape[1]
    pps = MOBA_PAGES_PER_STEP
    assert n_pages % pps == 0 and pps % (BLK // PAGE) == 0
    n_blk = n_pages * PAGE // BLK
    past_len = n_pages * PAGE
    assert past_len % BLK == 0 and n_new <= BLK and n_blk <= 128

    grid_spec = pltpu.PrefetchScalarGridSpec(
        num_scalar_prefetch=1,
        grid=(nseq, n_pages // pps),
        in_specs=[pl.BlockSpec(memory_space=pl.ANY),
                  pl.BlockSpec(memory_space=pl.ANY),
                  pl.BlockSpec((1, rows, KVD), lambda s, g, pt: (s, 0, 0)),
                  pl.BlockSpec((1, n_new, KVD), lambda s, g, pt: (s, 0, 0)),
                  pl.BlockSpec((1, n_new, KVD), lambda s, g, pt: (s, 0, 0)),
                  pl.BlockSpec(info.shape, lambda s, g, pt: (0, 0))],
        out_specs=pl.BlockSpec((1, rows // N_KV, KVD), lambda s, g, pt: (s, 0, 0)),
        scratch_shapes=[pltpu.VMEM((MOBA_PAGE_SLOTS, pps, KVD, PAGE), F32),
                        pltpu.VMEM((MOBA_PAGE_SLOTS, pps, KVD, PAGE), F32),
                        pltpu.SemaphoreType.DMA((MOBA_PAGE_SLOTS,)),
                        pltpu.SemaphoreType.DMA((MOBA_PAGE_SLOTS,)),
                        pltpu.VMEM((rows, 128), F32),
                        pltpu.VMEM((rows, 128), F32),
                        pltpu.VMEM((n_blk, rows, KVD), F32),
                        pltpu.VMEM((KVD, 128), F32)])
    return pl.pallas_call(
        functools.partial(_moba_sample_kernel, layer=layer, n_new=n_new, n_blk=n_blk,
                          past_len=past_len),
        grid_spec=grid_spec,
        out_shape=jax.ShapeDtypeStruct((nseq, rows // N_KV, KVD), F32),
        compiler_params=_params("arbitrary", "arbitrary"),
        name="moba_sample",
    )(page_table, poolT_k, poolT_v, qbd, kn, vn, info)


def _block_diag_queries(q, nseq, n_new):
    qr = q.reshape(nseq, n_new, N_KV, GROUP, HD).transpose(0, 2, 1, 3, 4)
    eye = jnp.eye(N_KV, dtype=q.dtype)
    qbd = qr[:, :, :, :, None, :] * eye[None, :, None, None, :, None]
    return qbd.reshape(nseq, N_KV * n_new * GROUP, KVD)


def _unfold_sample_out(o, nseq, n_new):
    o = o.reshape(nseq, n_new, GROUP, N_KV, HD).transpose(0, 1, 3, 2, 4)
    return o.reshape(nseq * n_new, QD).astype(BF16)


def _rows_major_view(cache):
    lead = cache.shape[:-3]
    nd = len(lead)
    perm = tuple(range(nd)) + (nd + 1, nd + 2, nd)
    return cache.transpose(perm).reshape(lead + (KVD, cache.shape[-3]))


def _sample_info(slopes_l2, sink_l2, n_new):
    def per_row(per_head):
        return jnp.broadcast_to(per_head.astype(F32)[:, None, :], (N_KV, n_new, GROUP)).reshape(-1)
    tt = jnp.broadcast_to(jnp.arange(n_new, dtype=F32)[None, :, None],
                          (N_KV, n_new, GROUP)).reshape(-1)
    return jnp.stack([per_row(slopes_l2), tt, per_row(sink_l2), jnp.zeros_like(tt)], axis=1)


def _swa_sample_attn(qs, ks, vs, cache_k, cache_v, slopes_l2, sink_l2):
    nseq = cache_k.shape[0]
    n_new = qs.shape[0] // nseq
    o = _swa_sample_call(_block_diag_queries(qs, nseq, n_new),
                         _rows_major_view(cache_k), _rows_major_view(cache_v),
                         ks.reshape(nseq, n_new, KVD), vs.reshape(nseq, n_new, KVD),
                         _sample_info(slopes_l2, sink_l2, n_new))
    return _unfold_sample_out(o, nseq, n_new)


def _moba_sample_attn(qs, ks, vs, poolT_k, poolT_v, layer, page_table, slopes_l2):
    nseq = page_table.shape[0]
    n_new = qs.shape[0] // nseq
    o = _moba_sample_call(page_table, poolT_k, poolT_v, layer,
                          _block_diag_queries(qs, nseq, n_new),
                          ks.reshape(nseq, n_new, KVD), vs.reshape(nseq, n_new, KVD),
                          _sample_info(slopes_l2, jnp.zeros_like(slopes_l2), n_new))
    return _unfold_sample_out(o, nseq, n_new)


def kernel(x_prompt, x_sample, cache_swa_k, cache_swa_v, cache_moba_k, cache_moba_v, page_table,
           g_attn, w_qkv, g_q, g_k, sinks, w_o, g_mlp, w_up, w_down):
    n, t, _ = x_prompt.shape
    nseq, n_new, _ = x_sample.shape
    depth = w_qkv.shape[0]
    nb = t // BLK
    assert t % BLK == 0

    head = jnp.arange(1, N_HEADS + 1, dtype=F32)
    slopes_l2 = jnp.exp2(-8.0 * head / N_HEADS).reshape(N_KV, GROUP) * LOG2E

    wqkv_b = w_qkv.astype(BF16)
    wo_b = w_o.astype(BF16)
    wup_b = w_up.astype(BF16)
    wdn_b = w_down.astype(BF16)
    poolT_k = _rows_major_view(cache_moba_k)
    poolT_v = _rows_major_view(cache_moba_v)

    ga = g_attn.reshape(depth, 1, D_MODEL)
    gm = g_mlp.reshape(depth, 1, D_MODEL)
    gq = jnp.tile(g_q, (1, N_KV)).reshape(depth, 1, KVD)
    gk = jnp.tile(g_k, (1, N_KV)).reshape(depth, 1, KVD)
    slope_rows = _prompt_rows(slopes_l2)

    def rows_first(xT):
        return xT.reshape(xT.shape[0], N_KV, HD, xT.shape[-1]).transpose(0, 3, 1, 2)

    def shifted_window(cache, new, wc):
        newT = new.reshape(nseq, n_new, KVD).transpose(0, 2, 1)
        winT = jnp.concatenate([_rows_major_view(cache), newT], axis=2)
        return rows_first(winT[:, :, winT.shape[2] - wc:])

    xp = x_prompt.reshape(n * t, D_MODEL)
    xs = x_sample.reshape(nseq * n_new, D_MODEL)
    swa_kp, swa_vp, swa_ks, swa_vs = [], [], [], []
    moba_kp, moba_vp, moba_ks, moba_vs = [], [], [], []
    for i in range(depth):
        is_swa = i % 2 == 0
        j = i // 2
        qT, kT, vT, katt, vT4, means = _qkv_prompt_call(xp, ga, wqkv_b, gq, gk, i, n, t)
        qs, ks, vs = _qkv_sample_call(xs, ga, wqkv_b, gq, gk, i)
        ks5 = ks.reshape(nseq, n_new, N_KV, HD)
        vs5 = vs.reshape(nseq, n_new, N_KV, HD)
        if is_swa:
            sink_l2 = sinks[j].reshape(N_KV, GROUP).astype(F32) * LOG2E
            attn_p = _swa_prompt_call(qT, katt, vT4, slope_rows, _prompt_rows(sink_l2))
            attn_s = _swa_sample_attn(qs, ks, vs, cache_swa_k[j], cache_swa_v[j],
                                      slopes_l2, sink_l2)
            w = min(WINDOW, t)
            wc = cache_swa_k.shape[2]
            swa_kp.append(rows_first(kT[:, :, t - w:]))
            swa_vp.append(rows_first(vT[:, :, t - w:]))
            swa_ks.append(shifted_window(cache_swa_k[j], ks, wc))
            swa_vs.append(shifted_window(cache_swa_v[j], vs, wc))
        else:
            attn_p = _moba_prompt_call(qT, katt, vT4, means.reshape(n, nb, KVD), slope_rows)
            attn_s = _moba_sample_attn(qs, ks, vs, poolT_k, poolT_v, j, page_table, slopes_l2)
            moba_kp.append(rows_first(kT))
            moba_vp.append(rows_first(vT))
            moba_ks.append(ks5)
            moba_vs.append(vs5)
        xp = _mlp_call(xp, attn_p.reshape(n * t, QD), wo_b, gm, wup_b, wdn_b, i)
        xs = _mlp_call(xs, attn_s, wo_b, gm, wup_b, wdn_b, i)
    return (xp.reshape(n, t, D_MODEL), xs.reshape(nseq, n_new, D_MODEL),
            jnp.stack(swa_kp), jnp.stack(swa_vp), jnp.stack(swa_ks), jnp.stack(swa_vs),
            jnp.stack(moba_kp), jnp.stack(moba_vp), jnp.stack(moba_ks), jnp.stack(moba_vs))
```

```python
import functools
import math

import jax
import jax.numpy as jnp
from jax import lax
from jax.experimental import pallas as pl
from jax.experimental.pallas import tpu as pltpu

F32 = jnp.float32
BF16 = jnp.bfloat16

D_MODEL = 1024
N_HEADS = 16
N_KV = 4
GROUP = N_HEADS // N_KV
HD = D_MODEL // N_HEADS
QD = N_HEADS * HD
KVD = N_KV * HD
QKV_DIM = QD + 2 * KVD
D_FF = 4 * D_MODEL
WINDOW = 128
BLK = 256
TOPK = 3
PAGE = 128
NEW_TILE = 128
EPS = 1e-6
NEG = -1e30
LOG2E = math.log2(math.e)

KAUG = 2 * HD
VAUG = HD + 16
N_SLOPE_PARTS = 3
LANES = GROUP * BLK

ROW_TILE = 512
FF_CHUNK = 1024
SWA_SEQS_PER_STEP = 8
MOBA_PAGES_PER_STEP = 16
MOBA_PAGE_SLOTS = 4
VMEM_LIMIT = 56 * 1024 * 1024


def _params(*sem):
    return pltpu.CompilerParams(dimension_semantics=sem, vmem_limit_bytes=VMEM_LIMIT)


def _split_bf16(x):
    hi = x.astype(BF16)
    lo = (x - hi.astype(F32)).astype(BF16)
    return hi, lo


def _project_qkv(x_ref, g_ref, w_ref, gq_ref, gk_ref):
    x = x_ref[...]
    ms = jnp.mean(x * x, axis=-1, keepdims=True)
    xn = (x * lax.rsqrt(ms + EPS)) * g_ref[...]
    qkv = jnp.dot(xn.astype(BF16), w_ref[...], preferred_element_type=F32)

    r = lax.broadcasted_iota(jnp.int32, (KVD, KVD), 0) // HD
    c = lax.broadcasted_iota(jnp.int32, (KVD, KVD), 1) // HD
    seg = jnp.where(r == c, 1.0, 0.0).astype(BF16)

    def head_norm(z, gain):
        hi, lo = _split_bf16(z * z)
        tot = (jnp.dot(hi, seg, preferred_element_type=F32)
               + jnp.dot(lo, seg, preferred_element_type=F32))
        return (z * lax.rsqrt(tot * (1.0 / HD) + EPS)) * gain

    q = [head_norm(qkv[:, c * KVD:(c + 1) * KVD], gq_ref[...]) * (HD ** -0.5 * LOG2E)
         for c in range(QD // KVD)]
    k = head_norm(qkv[:, QD:QD + KVD], gk_ref[...])
    return q, k, qkv[:, QD + KVD:]


def _qkv_sample_kernel(x_ref, g_ref, w_ref, gq_ref, gk_ref, q_ref, k_ref, v_ref):
    q, k, v = _project_qkv(x_ref, g_ref, w_ref, gq_ref, gk_ref)
    for c, qc in enumerate(q):
        q_ref[:, c * KVD:(c + 1) * KVD] = qc.astype(q_ref.dtype)
    k_ref[...] = k
    v_ref[...] = v


def _qkv_prompt_kernel(x_ref, g_ref, w_ref, gq_ref, gk_ref,
                       qT_ref, kT_ref, vT_ref, katt_ref, vT4_ref, mean_ref):
    q, k, v = _project_qkv(x_ref, g_ref, w_ref, gq_ref, gk_ref)
    rows = k.shape[0]
    for c, qc in enumerate(q):
        qT_ref[0, c * KVD:(c + 1) * KVD, :] = qc.T.astype(qT_ref.dtype)
    kT_ref[0] = k.T
    vT = v.T
    vT_ref[0] = vT

    lane = lax.broadcasted_iota(jnp.int32, (rows, 2 * HD), 1)
    key_off = (lax.broadcasted_iota(jnp.int32, (rows, 2 * HD), 0) % BLK).astype(F32)
    aug = jnp.where((lane >= HD) & (lane < HD + N_SLOPE_PARTS), key_off, 0.0)
    groups = []
    for pair in range(N_KV // 2):
        two = k[:, pair * 2 * HD:(pair + 1) * 2 * HD]
        groups.append(jnp.where(lane < HD, two, aug))
        groups.append(jnp.where(lane < HD, pltpu.roll(two, HD, axis=1), aug))
    katt = jnp.concatenate(groups, axis=1).astype(katt_ref.dtype)
    ones_rows = jnp.where(lax.broadcasted_iota(jnp.int32, (VAUG - HD, BLK), 0) == 0, 1.0, 0.0)
    for b in range(rows // BLK):
        katt_ref[0, b] = katt[b * BLK:(b + 1) * BLK]
        vblk = vT[:, b * BLK:(b + 1) * BLK]
        vT4_ref[0, b] = jnp.concatenate(
            [piece for kv in range(N_KV) for piece in (vblk[kv * HD:(kv + 1) * HD], ones_rows)],
            axis=0).astype(vT4_ref.dtype)
        mean_ref[0, b:b + 1, :] = jnp.sum(
            k[b * BLK:(b + 1) * BLK], axis=0, keepdims=True) * (1.0 / BLK)


def _layer_spec(shape, layer):
    nd = len(shape)
    return pl.BlockSpec((None,) + tuple(shape), lambda *_: (layer,) + (0,) * nd,
                        pipeline_mode=pl.Buffered(1))


def _qkv_in_specs(x_spec, layer):
    return [x_spec,
            _layer_spec((1, D_MODEL), layer),
            _layer_spec((D_MODEL, QKV_DIM), layer),
            _layer_spec((1, KVD), layer),
            _layer_spec((1, KVD), layer)]


def _qkv_sample_call(x, g_attn, wqkv, gq, gk, layer):
    rows = x.shape[0]
    assert rows % ROW_TILE == 0
    return pl.pallas_call(
        _qkv_sample_kernel,
        grid=(rows // ROW_TILE,),
        in_specs=_qkv_in_specs(pl.BlockSpec((ROW_TILE, D_MODEL), lambda i: (i, 0)), layer),
        out_specs=[pl.BlockSpec((ROW_TILE, QD), lambda i: (i, 0)),
                   pl.BlockSpec((ROW_TILE, KVD), lambda i: (i, 0)),
                   pl.BlockSpec((ROW_TILE, KVD), lambda i: (i, 0))],
        out_shape=[jax.ShapeDtypeStruct((rows, QD), BF16),
                   jax.ShapeDtypeStruct((rows, KVD), F32),
                   jax.ShapeDtypeStruct((rows, KVD), F32)],
        compiler_params=_params("arbitrary"),
        name="qkv_sample",
    )(x, g_attn, wqkv, gq, gk)


def _qkv_prompt_call(x, g_attn, wqkv, gq, gk, layer, n, t):
    assert t % ROW_TILE == 0 and ROW_TILE % BLK == 0
    steps = t // ROW_TILE
    bpt = ROW_TILE // BLK
    nb = t // BLK
    return pl.pallas_call(
        _qkv_prompt_kernel,
        grid=(n, steps),
        in_specs=_qkv_in_specs(
            pl.BlockSpec((ROW_TILE, D_MODEL), lambda s, i: (s * steps + i, 0)), layer),
        out_specs=[pl.BlockSpec((1, QD, ROW_TILE), lambda s, i: (s, 0, i)),
                   pl.BlockSpec((1, KVD, ROW_TILE), lambda s, i: (s, 0, i)),
                   pl.BlockSpec((1, KVD, ROW_TILE), lambda s, i: (s, 0, i)),
                   pl.BlockSpec((1, bpt, BLK, N_KV * KAUG), lambda s, i: (s, i, 0, 0)),
                   pl.BlockSpec((1, bpt, N_KV * VAUG, BLK), lambda s, i: (s, i, 0, 0)),
                   pl.BlockSpec((1, bpt, KVD), lambda s, i: (s * steps + i, 0, 0))],
        out_shape=[jax.ShapeDtypeStruct((n, QD, t), BF16),
                   jax.ShapeDtypeStruct((n, KVD, t), F32),
                   jax.ShapeDtypeStruct((n, KVD, t), F32),
                   jax.ShapeDtypeStruct((n, nb, BLK, N_KV * KAUG), BF16),
                   jax.ShapeDtypeStruct((n, nb, N_KV * VAUG, BLK), BF16),
                   jax.ShapeDtypeStruct((n * steps, bpt, KVD), F32)],
        compiler_params=_params("arbitrary", "arbitrary"),
        name="qkv_prompt",
    )(x, g_attn, wqkv, gq, gk)


def _mlp_kernel(x_ref, a_ref, wo_ref, g_ref, wup_ref, wdn_ref, o_ref):
    x1 = x_ref[...] + jnp.dot(a_ref[...], wo_ref[...], preferred_element_type=F32)
    ms = jnp.mean(x1 * x1, axis=-1, keepdims=True)
    h = ((x1 * lax.rsqrt(ms + EPS)) * g_ref[...]).astype(BF16)
    acc = x1
    for c in range(D_FF // FF_CHUNK):
        u = jnp.dot(h, wup_ref[:, c * FF_CHUNK:(c + 1) * FF_CHUNK], preferred_element_type=F32)
        u = jnp.square(jnp.maximum(u, 0.0)).astype(BF16)
        acc = acc + jnp.dot(u, wdn_ref[c * FF_CHUNK:(c + 1) * FF_CHUNK, :],
                            preferred_element_type=F32)
    o_ref[...] = acc


def _mlp_call(x, attn, wo, g_mlp, wup, wdn, layer):
    rows = x.shape[0]
    assert rows % ROW_TILE == 0
    return pl.pallas_call(
        _mlp_kernel,
        grid=(rows // ROW_TILE,),
        in_specs=[pl.BlockSpec((ROW_TILE, D_MODEL), lambda i: (i, 0)),
                  pl.BlockSpec((ROW_TILE, QD), lambda i: (i, 0)),
                  _layer_spec((QD, D_MODEL), layer),
                  _layer_spec((1, D_MODEL), layer),
                  _layer_spec((D_MODEL, D_FF), layer),
                  _layer_spec((D_FF, D_MODEL), layer)],
        out_specs=pl.BlockSpec((ROW_TILE, D_MODEL), lambda i: (i, 0)),
        out_shape=jax.ShapeDtypeStruct((rows, D_MODEL), F32),
        compiler_params=_params("arbitrary"),
        name="oproj_mlp",
    )(x, attn, wo, g_mlp, wup, wdn)


def _aug_queries(qT_ref, slope, kv):
    qs = jnp.concatenate(
        [qT_ref[0, (kv * GROUP + g) * HD:(kv * GROUP + g + 1) * HD, :] for g in range(GROUP)],
        axis=1)
    row = lax.broadcasted_iota(jnp.int32, (KAUG - HD, LANES), 0)
    aug = jnp.zeros((KAUG - HD, LANES), F32)
    rest = slope
    for part in range(N_SLOPE_PARTS):
        piece = rest.astype(BF16).astype(F32)
        aug = jnp.where(row == part, piece, aug)
        rest = rest - piece
    return qs, jnp.concatenate([qs, aug.astype(BF16)], axis=0)


def _query_minus_key_offset():
    rr = lax.broadcasted_iota(jnp.int32, (1, LANES), 1) % BLK
    cc = lax.broadcasted_iota(jnp.int32, (BLK, 1), 0)
    return rr - cc


def _store_heads(o_ref, kv, o):
    for pair in range(GROUP // 2):
        two = jnp.concatenate([o[:, (2 * pair) * BLK:(2 * pair + 1) * BLK],
                               o[:, (2 * pair + 1) * BLK:(2 * pair + 2) * BLK]], axis=0)
        lane0 = (kv * GROUP + 2 * pair) * HD
        o_ref[0, :, lane0:lane0 + 2 * HD] = two.T.astype(o_ref.dtype)


def _swa_prompt_kernel(qT_ref, kc_ref, kp_ref, vc_ref, vp_ref, slope_ref, sink_ref, o_ref):
    b = pl.program_id(1)
    d = _query_minus_key_offset()
    cur_ok = (d >= 0) & (d < WINDOW)
    prev_ok = d < WINDOW - BLK
    prev_bias = jnp.where(b > 0, 0.0, NEG)
    rr = (lax.broadcasted_iota(jnp.int32, (1, LANES), 1) % BLK).astype(F32)
    scores = []
    for kv in range(N_KV):
        _, qaug = _aug_queries(qT_ref, slope_ref[kv], kv)
        scores.append((jnp.dot(kc_ref[0, 0, :, kv * KAUG:(kv + 1) * KAUG], qaug,
                               preferred_element_type=F32),
                       jnp.dot(kp_ref[0, 0, :, kv * KAUG:(kv + 1) * KAUG], qaug,
                               preferred_element_type=F32)))
    for kv in range(N_KV):
        slope = slope_ref[kv]
        sink = sink_ref[kv] + slope * rr
        prev_off = prev_bias - slope * float(BLK)
        s1 = jnp.where(cur_ok, scores[kv][0], NEG)
        s2 = jnp.where(prev_ok, scores[kv][1], NEG)
        m = jnp.maximum(jnp.maximum(jnp.max(s1, axis=0, keepdims=True),
                                    jnp.max(s2, axis=0, keepdims=True) + prev_off), sink)
        p1 = jnp.exp2(s1 - m).astype(BF16)
        p2 = jnp.exp2(s2 + (prev_off - m)).astype(BF16)
        acc = (jnp.dot(vc_ref[0, 0, kv * VAUG:(kv + 1) * VAUG, :], p1,
                       preferred_element_type=F32)
               + jnp.dot(vp_ref[0, 0, kv * VAUG:(kv + 1) * VAUG, :], p2,
                         preferred_element_type=F32))
        denom = acc[HD:HD + 1, :] + jnp.exp2(sink - m)
        _store_heads(o_ref, kv, acc[:HD, :] / denom)


def _swa_prompt_call(qT, katt, vT4, slope_rows, sink_rows):
    n, _, t = qT.shape
    nb = t // BLK
    cur = lambda i, b: (i, b, 0, 0)
    prev = lambda i, b: (i, jnp.maximum(b - 1, 0), 0, 0)
    whole = lambda i, b: (0, 0, 0)
    return pl.pallas_call(
        _swa_prompt_kernel,
        grid=(n, nb),
        in_specs=[pl.BlockSpec((1, QD, BLK), lambda i, b: (i, 0, b)),
                  pl.BlockSpec((1, 1, BLK, N_KV * KAUG), cur),
                  pl.BlockSpec((1, 1, BLK, N_KV * KAUG), prev),
                  pl.BlockSpec((1, 1, N_KV * VAUG, BLK), cur),
                  pl.BlockSpec((1, 1, N_KV * VAUG, BLK), prev),
                  pl.BlockSpec(slope_rows.shape, whole),
                  pl.BlockSpec(sink_rows.shape, whole)],
        out_specs=pl.BlockSpec((1, BLK, QD), lambda i, b: (i, b, 0)),
        out_shape=jax.ShapeDtypeStruct((n, t, QD), BF16),
        compiler_params=_params("arbitrary", "arbitrary"),
        name="swa_prompt",
    )(qT, katt, katt, vT4, vT4, slope_rows, sink_rows)


def _moba_prompt_kernel(qT_ref, k_ref, vT_ref, mean_ref, slope_ref, o_ref,
                        qaug_ref, sel_ref, m_ref, acc_ref, s_ref, p_ref, *, nb):
    assert N_KV == 4
    b = pl.program_id(1)
    causal = _query_minus_key_offset() >= 0
    jj = lax.broadcasted_iota(jnp.int32, (nb, LANES), 0)
    mean_hi, mean_lo = _split_bf16(mean_ref[0])

    def score_stage(kv, j):
        return jnp.dot(k_ref[0, j, :, kv * KAUG:(kv + 1) * KAUG], qaug_ref[kv],
                       preferred_element_type=F32)

    def value_stage(kv, j, p):
        acc_ref[kv] += jnp.dot(vT_ref[0, j, kv * VAUG:(kv + 1) * VAUG, :], p,
                               preferred_element_type=F32)

    def own_softmax_stage(kv, s):
        s = jnp.where(causal, s, NEG)
        m = jnp.max(s, axis=0, keepdims=True)
        m_ref[kv] = m
        acc_ref[kv] = jnp.zeros((VAUG, LANES), F32)
        return jnp.exp2(s - m).astype(BF16)

    def past_softmax_stage(kv, j, dist):
        s = s_ref[kv]
        bias = jnp.where(sel_ref[kv, pl.ds(j, 1), :] > 0.5, -dist * slope_ref[kv], NEG)
        m_old = m_ref[kv]
        m_new = jnp.maximum(m_old, jnp.max(s, axis=0, keepdims=True) + bias)
        acc_ref[kv] = jnp.exp2(m_old - m_new) * acc_ref[kv]
        m_ref[kv] = m_new
        return jnp.exp2(s + (bias - m_new)).astype(BF16)

    for kv in range(N_KV):
        qs, qaug = _aug_queries(qT_ref, slope_ref[kv], kv)
        qaug_ref[kv] = qaug

        gate = (jnp.dot(mean_hi[:, kv * HD:(kv + 1) * HD], qs, preferred_element_type=F32)
                + jnp.dot(mean_lo[:, kv * HD:(kv + 1) * HD], qs, preferred_element_type=F32))
        gate = jnp.where(jj < b, gate, NEG)
        sel = jnp.zeros((nb, LANES), F32)
        for _ in range(TOPK):
            mx = jnp.max(gate, axis=0, keepdims=True)
            idx = jnp.min(jnp.where(gate == mx, jj, nb), axis=0, keepdims=True)
            hit = jj == jnp.where(mx > 0.5 * NEG, idx, -1)
            sel = jnp.where(hit, 1.0, sel)
            gate = jnp.where(hit, NEG, gate)
        sel_ref[kv] = sel

    s_own = [score_stage(0, b), score_stage(1, b)]
    p_own = []
    for kv in range(N_KV):
        p_own.append(own_softmax_stage(kv, s_own[kv]))
        if kv + 2 < N_KV:
            s_own.append(score_stage(kv + 2, b))
        else:
            s_ref[kv + 2 - N_KV] = score_stage(kv + 2 - N_KV, 0)
        if kv >= 1:
            value_stage(kv - 1, b, p_own[kv - 1])
    value_stage(N_KV - 1, b, p_own[N_KV - 1])
    p_ref[...] = jnp.zeros_like(p_ref)

    def past_block(j, carry):
        dist = ((b - j) * BLK).astype(F32)
        j_next = jnp.minimum(j + 1, b - 1)
        j_prev = jnp.maximum(j - 1, 0)
        p0 = past_softmax_stage(0, j, dist)
        value_stage(3, j_prev, p_ref[...])
        s_ref[2] = score_stage(2, j)
        p1 = past_softmax_stage(1, j, dist)
        value_stage(0, j, p0)
        s_ref[3] = score_stage(3, j)
        p2 = past_softmax_stage(2, j, dist)
        value_stage(1, j, p1)
        s_ref[0] = score_stage(0, j_next)
        p_ref[...] = past_softmax_stage(3, j, dist)
        value_stage(2, j, p2)
        s_ref[1] = score_stage(1, j_next)
        return carry

    def two_past_blocks(i, carry):
        return past_block(2 * i + 1, past_block(2 * i, carry))

    lax.fori_loop(0, b // 2, two_past_blocks, 0)
    lax.fori_loop(b // 2 * 2, b, past_block, 0)
    value_stage(3, jnp.maximum(b - 1, 0), p_ref[...])
    for kv in range(N_KV):
        _store_heads(o_ref, kv, acc_ref[kv, :HD, :] / acc_ref[kv, HD:HD + 1, :])


def _moba_prompt_call(qT, katt, vT4, means, slope_rows):
    n, _, t = qT.shape
    nb = t // BLK
    whole = lambda i, b: (0, 0, 0)
    return pl.pallas_call(
        functools.partial(_moba_prompt_kernel, nb=nb),
        grid=(n, nb),
        in_specs=[pl.BlockSpec((1, QD, BLK), lambda i, b: (i, 0, b)),
                  pl.BlockSpec((1, nb, BLK, N_KV * KAUG), lambda i, b: (i, 0, 0, 0)),
                  pl.BlockSpec((1, nb, N_KV * VAUG, BLK), lambda i, b: (i, 0, 0, 0)),
                  pl.BlockSpec((1, nb, KVD), lambda i, b: (i, 0, 0)),
                  pl.BlockSpec(slope_rows.shape, whole)],
        out_specs=pl.BlockSpec((1, BLK, QD), lambda i, b: (i, b, 0)),
        out_shape=jax.ShapeDtypeStruct((n, t, QD), BF16),
        scratch_shapes=[pltpu.VMEM((N_KV, KAUG, LANES), BF16),
                        pltpu.VMEM((N_KV, nb, LANES), F32),
                        pltpu.VMEM((N_KV, 1, LANES), F32),
                        pltpu.VMEM((N_KV, VAUG, LANES), F32),
                        pltpu.VMEM((N_KV, BLK, LANES), F32),
                        pltpu.VMEM((BLK, LANES), BF16)],
        compiler_params=_params("arbitrary", "arbitrary"),
        name="moba_prompt",
    )(qT, katt, vT4, means, slope_rows)


def _prompt_rows(per_head):
    return jnp.repeat(per_head.astype(F32), BLK, axis=1).reshape(N_KV, 1, LANES)


def _new_token_scores(qb, knew, slope, tt, n_new):
    qf = qb.astype(F32)
    out = []
    for t in range(n_new):
        s = jnp.sum(qf * knew[t:t + 1, :], axis=1, keepdims=True)
        d = tt - float(t)
        out.append(jnp.where(d >= 0.0, s - slope * d, NEG))
    return out


def _fold_heads(o):
    rows = o.shape[0]
    step = rows // N_KV
    r = lax.broadcasted_iota(jnp.int32, (rows, KVD), 0) // step
    c = lax.broadcasted_iota(jnp.int32, (rows, KVD), 1) // HD
    o = jnp.where(r == c, o, 0.0)
    return o[0:step] + o[step:2 * step] + o[2 * step:3 * step] + o[3 * step:4 * step]


_NT = (((1,), (1,)), ((), ()))


def _swa_sample_kernel(qbd_ref, kc_ref, vc_ref, kn_ref, vn_ref, info_ref, o_ref, *, n_new):
    slope = info_ref[:, 0:1]
    tt = info_ref[:, 1:2]
    sink = info_ref[:, 2:3]
    wc = kc_ref.shape[2]
    cidx = lax.broadcasted_iota(jnp.int32, (1, wc), 1).astype(F32)
    delta = (float(wc) + tt) - cidx
    nidx = lax.broadcasted_iota(jnp.int32, (1, NEW_TILE), 1).astype(F32)
    delta_new = tt - nidx
    new_ok = (delta_new >= 0.0) & (nidx < float(n_new))
    seqs = range(qbd_ref.shape[0])
    scores = [(jnp.dot(qbd_ref[s_i], kc_ref[s_i].astype(BF16), preferred_element_type=F32),
               jnp.dot(qbd_ref[s_i], kn_ref[s_i], preferred_element_type=F32))
              for s_i in seqs]
    probs = []
    for s_i in seqs:
        s = jnp.where(delta < float(WINDOW), scores[s_i][0] - slope * delta, NEG)
        s_new = jnp.where(new_ok, scores[s_i][1] - slope * delta_new, NEG)
        m = jnp.maximum(jnp.maximum(jnp.max(s, axis=1, keepdims=True),
                                    jnp.max(s_new, axis=1, keepdims=True)), sink)
        p = jnp.exp2(s - m)
        p_new = jnp.exp2(s_new - m)
        denom = (jnp.sum(p, axis=1, keepdims=True) + jnp.sum(p_new, axis=1, keepdims=True)
                 + jnp.exp2(sink - m))
        probs.append((p.astype(BF16), p_new.astype(BF16), denom))
    for s_i in seqs:
        p, p_new, denom = probs[s_i]
        o = (lax.dot_general(p, vc_ref[s_i].astype(BF16), _NT, preferred_element_type=F32)
             + lax.dot_general(p_new, vn_ref[s_i], _NT, preferred_element_type=F32))
        o_ref[s_i] = _fold_heads(o / denom)


def _swa_sample_call(qbd, kcT, vcT, knT, vnT, info, n_new):
    nseq, rows, _ = qbd.shape
    wc = kcT.shape[2]
    sb = SWA_SEQS_PER_STEP
    assert nseq % sb == 0 and n_new <= NEW_TILE
    return pl.pallas_call(
        functools.partial(_swa_sample_kernel, n_new=n_new),
        grid=(nseq // sb,),
        in_specs=[pl.BlockSpec((sb, rows, KVD), lambda i: (i, 0, 0)),
                  pl.BlockSpec((sb, KVD, wc), lambda i: (i, 0, 0)),
                  pl.BlockSpec((sb, KVD, wc), lambda i: (i, 0, 0)),
                  pl.BlockSpec((sb, KVD, NEW_TILE), lambda i: (i, 0, 0)),
                  pl.BlockSpec((sb, KVD, NEW_TILE), lambda i: (i, 0, 0)),
                  pl.BlockSpec(info.shape, lambda i: (0, 0))],
        out_specs=pl.BlockSpec((sb, rows // N_KV, KVD), lambda i: (i, 0, 0)),
        out_shape=jax.ShapeDtypeStruct((nseq, rows // N_KV, KVD), F32),
        compiler_params=_params("arbitrary"),
        name="swa_sample",
    )(qbd, kcT, vcT, knT, vnT, info)


def _moba_sample_kernel(pt_ref, kpool_ref, vpool_ref, qbd_ref, kn_ref, vn_ref, info_ref, o_ref,
                        kbuf_ref, vbuf_ref, ksem, vsem, m_ref, l_ref, oall_ref, meanT_ref,
                        *, layer, n_new, n_blk, past_len):
    pps = MOBA_PAGES_PER_STEP
    seq = pl.program_id(0)
    step = pl.program_id(1)
    steps = pl.num_programs(1)
    unit = seq * steps + step
    n_units = pl.num_programs(0) * steps
    slot = unit % MOBA_PAGE_SLOTS

    def page_copies(page, page_slot, i):
        return (pltpu.make_async_copy(kpool_ref.at[layer, page], kbuf_ref.at[page_slot, i],
                                      ksem.at[page_slot]),
                pltpu.make_async_copy(vpool_ref.at[layer, page], vbuf_ref.at[page_slot, i],
                                      vsem.at[page_slot]))

    def start_unit(u, page_slot):
        for i in range(pps):
            page = pt_ref[u // steps, (u % steps) * pps + i]
            for copy in page_copies(page, page_slot, i):
                copy.start(priority=i % 2)

    ahead = MOBA_PAGE_SLOTS - 1

    @pl.when(unit == 0)
    def _():
        for u in range(ahead):
            start_unit(u, u)

    @pl.when(unit + ahead < n_units)
    def _():
        start_unit(unit + ahead, (unit + ahead) % MOBA_PAGE_SLOTS)

    for i in range(pps):
        for copy in page_copies(0, slot, i):
            copy.wait()
    k_refs = [kbuf_ref.at[slot, i] for i in range(pps)]
    v_refs = [vbuf_ref.at[slot, i] for i in range(pps)]

    slope = info_ref[:, 0:1]
    tt = info_ref[:, 1:2]
    qb = qbd_ref[0]
    rows = qb.shape[0]
    lane = lax.broadcasted_iota(jnp.int32, (rows, 128), 1)
    mlane = lax.broadcasted_iota(jnp.int32, (KVD, 128), 1)
    cidx = lax.broadcasted_iota(jnp.int32, (1, BLK), 1).astype(F32)
    pen0 = slope * (tt - cidx)

    @pl.when(step == 0)
    def _():
        m_ref[...] = jnp.zeros_like(m_ref)
        l_ref[...] = jnp.zeros_like(l_ref)
        meanT_ref[...] = jnp.zeros_like(meanT_ref)

    ppb = BLK // PAGE
    blocks = range(pps // ppb)
    scores = []
    for jb in blocks:
        kp = [k_refs[jb * ppb + i][...] for i in range(ppb)]
        scores.append(jnp.dot(qb, jnp.concatenate(kp, axis=1).astype(BF16),
                              preferred_element_type=F32))
    means, m_new, l_new = meanT_ref[...], m_ref[...], l_ref[...]
    probs = []
    for jb in blocks:
        j = step * (pps // ppb) + jb
        ksum = k_refs[jb * ppb][...]
        for i in range(1, ppb):
            ksum = ksum + k_refs[jb * ppb + i][...]
        kmean = jnp.sum(ksum, axis=1, keepdims=True) * (1.0 / BLK)
        means = jnp.where(mlane == j, kmean, means)
        s = scores[jb] - pen0
        mj = jnp.max(s, axis=1, keepdims=True)
        p = jnp.exp2(s - mj)
        m_new = jnp.where(lane == j, mj, m_new)
        l_new = jnp.where(lane == j, jnp.sum(p, axis=1, keepdims=True), l_new)
        probs.append(p.astype(BF16))
    meanT_ref[...], m_ref[...], l_ref[...] = means, m_new, l_new
    for jb in blocks:
        j = step * (pps // ppb) + jb
        vp = [v_refs[jb * ppb + i][...] for i in range(ppb)]
        oall_ref[j] = lax.dot_general(probs[jb], jnp.concatenate(vp, axis=1).astype(BF16), _NT,
                                      preferred_element_type=F32)

    @pl.when(step == pl.num_programs(1) - 1)
    def _():
        kn = kn_ref[0]
        vn = vn_ref[0]
        mean_hi, mean_lo = _split_bf16(meanT_ref[...])
        gate = (jnp.dot(qb, mean_hi, preferred_element_type=F32)
                + jnp.dot(qb, mean_lo, preferred_element_type=F32))[:, :n_blk]
        bl = lax.broadcasted_iota(jnp.int32, (rows, n_blk), 1)
        sel = jnp.zeros((rows, n_blk), jnp.bool_)
        for _ in range(min(TOPK, n_blk)):
            mx = jnp.max(gate, axis=1, keepdims=True)
            idx = jnp.min(jnp.where(gate == mx, bl, n_blk), axis=1, keepdims=True)
            hit = bl == idx
            sel = sel | hit
            gate = jnp.where(hit, NEG, gate)
        off = slope * (float(past_len) - bl.astype(F32) * float(BLK))
        mt = jnp.where(sel, m_ref[:, :n_blk] - off, NEG)
        s_new = _new_token_scores(qb, kn, slope, tt, n_new)
        m = jnp.max(mt, axis=1, keepdims=True)
        for sn in s_new:
            m = jnp.maximum(m, sn)
        w = jnp.where(sel, jnp.exp2(mt - m), 0.0)
        denom = jnp.sum(w * l_ref[:, :n_blk], axis=1, keepdims=True)
        o = jnp.zeros((rows, KVD), F32)
        for t, sn in enumerate(s_new):
            e = jnp.exp2(sn - m)
            denom = denom + e
            o = o + e * vn[t:t + 1, :]
        for j in range(n_blk):
            wj = jnp.sum(jnp.where(bl == j, w, 0.0), axis=1, keepdims=True)
            o = o + wj * oall_ref[j]
        o_ref[0] = _fold_heads(o / denom)


def _moba_sample_call(page_table, poolT_k, poolT_v, layer, qbd, kn, vn, info):
    nseq, rows, _ = qbd.shape
    n_pages = page_table.shape[1]
    n_new = kn.shape[1]
    pps = MOBA_PAGES_PER_STEP
    assert n_pages % pps == 0 and pps % (BLK // PAGE) == 0
    n_blk = n_pages * PAGE // BLK
    past_len = n_pages * PAGE
    assert past_len % BLK == 0 and n_new <= BLK and n_blk <= 128

    grid_spec = pltpu.PrefetchScalarGridSpec(
        num_scalar_prefetch=1,
        grid=(nseq, n_pages // pps),
        in_specs=[pl.BlockSpec(memory_space=pl.ANY),
                  pl.BlockSpec(memory_space=pl.ANY),
                  pl.BlockSpec((1, rows, KVD), lambda s, g, pt: (s, 0, 0)),
                  pl.BlockSpec((1, n_new, KVD), lambda s, g, pt: (s, 0, 0)),
                  pl.BlockSpec((1, n_new, KVD), lambda s, g, pt: (s, 0, 0)),
                  pl.BlockSpec(info.shape, lambda s, g, pt: (0, 0))],
        out_specs=pl.BlockSpec((1, rows // N_KV, KVD), lambda s, g, pt: (s, 0, 0)),
        scratch_shapes=[pltpu.VMEM((MOBA_PAGE_SLOTS, pps, KVD, PAGE), F32),
                        pltpu.VMEM((MOBA_PAGE_SLOTS, pps, KVD, PAGE), F32),
                        pltpu.SemaphoreType.DMA((MOBA_PAGE_SLOTS,)),
                        pltpu.SemaphoreType.DMA((MOBA_PAGE_SLOTS,)),
                        pltpu.VMEM((rows, 128), F32),
                        pltpu.VMEM((rows, 128), F32),
                        pltpu.VMEM((n_blk, rows, KVD), F32),
                        pltpu.VMEM((KVD, 128), F32)])
    return pl.pallas_call(
        functools.partial(_moba_sample_kernel, layer=layer, n_new=n_new, n_blk=n_blk,
                          past_len=past_len),
        grid_spec=grid_spec,
        out_shape=jax.ShapeDtypeStruct((nseq, rows // N_KV, KVD), F32),
        compiler_params=_params("arbitrary", "arbitrary"),
        name="moba_sample",
    )(page_table, poolT_k, poolT_v, qbd, kn, vn, info)


def _block_diag_queries(q, nseq, n_new):
    qr = q.reshape(nseq, n_new, N_KV, GROUP, HD).transpose(0, 2, 1, 3, 4)
    eye = jnp.eye(N_KV, dtype=q.dtype)
    qbd = qr[:, :, :, :, None, :] * eye[None, :, None, None, :, None]
    return qbd.reshape(nseq, N_KV * n_new * GROUP, KVD)


def _unfold_sample_out(o, nseq, n_new):
    o = o.reshape(nseq, n_new, GROUP, N_KV, HD).transpose(0, 1, 3, 2, 4)
    return o.reshape(nseq * n_new, QD).astype(BF16)


def _rows_major_view(cache):
    lead = cache.shape[:-3]
    nd = len(lead)
    perm = tuple(range(nd)) + (nd + 1, nd + 2, nd)
    return cache.transpose(perm).reshape(lead + (KVD, cache.shape[-3]))


def _sample_info(slopes_l2, sink_l2, n_new):
    def per_row(per_head):
        return jnp.broadcast_to(per_head.astype(F32)[:, None, :], (N_KV, n_new, GROUP)).reshape(-1)
    tt = jnp.broadcast_to(jnp.arange(n_new, dtype=F32)[None, :, None],
                          (N_KV, n_new, GROUP)).reshape(-1)
    return jnp.stack([per_row(slopes_l2), tt, per_row(sink_l2), jnp.zeros_like(tt)], axis=1)


def _swa_sample_attn(qs, ks, vs, cache_k, cache_v, slopes_l2, sink_l2):
    nseq = cache_k.shape[0]
    n_new = qs.shape[0] // nseq
    def new_tile(x):
        xT = x.astype(BF16).reshape(nseq, n_new, KVD).transpose(0, 2, 1)
        return jnp.pad(xT, ((0, 0), (0, 0), (0, NEW_TILE - n_new)))

    o = _swa_sample_call(_block_diag_queries(qs, nseq, n_new),
                         _rows_major_view(cache_k), _rows_major_view(cache_v),
                         new_tile(ks), new_tile(vs),
                         _sample_info(slopes_l2, sink_l2, n_new), n_new)
    return _unfold_sample_out(o, nseq, n_new)


def _moba_sample_attn(qs, ks, vs, poolT_k, poolT_v, layer, page_table, slopes_l2):
    nseq = page_table.shape[0]
    n_new = qs.shape[0] // nseq
    o = _moba_sample_call(page_table, poolT_k, poolT_v, layer,
                          _block_diag_queries(qs, nseq, n_new),
                          ks.reshape(nseq, n_new, KVD), vs.reshape(nseq, n_new, KVD),
                          _sample_info(slopes_l2, jnp.zeros_like(slopes_l2), n_new))
    return _unfold_sample_out(o, nseq, n_new)


def kernel(x_prompt, x_sample, cache_swa_k, cache_swa_v, cache_moba_k, cache_moba_v, page_table,
           g_attn, w_qkv, g_q, g_k, sinks, w_o, g_mlp, w_up, w_down):
    n, t, _ = x_prompt.shape
    nseq, n_new, _ = x_sample.shape
    depth = w_qkv.shape[0]
    nb = t // BLK
    assert t % BLK == 0

    head = jnp.arange(1, N_HEADS + 1, dtype=F32)
    slopes_l2 = jnp.exp2(-8.0 * head / N_HEADS).reshape(N_KV, GROUP) * LOG2E

    wqkv_b = w_qkv.astype(BF16)
    wo_b = w_o.astype(BF16)
    wup_b = w_up.astype(BF16)
    wdn_b = w_down.astype(BF16)
    poolT_k = _rows_major_view(cache_moba_k)
    poolT_v = _rows_major_view(cache_moba_v)

    ga = g_attn.reshape(depth, 1, D_MODEL)
    gm = g_mlp.reshape(depth, 1, D_MODEL)
    gq = jnp.tile(g_q, (1, N_KV)).reshape(depth, 1, KVD)
    gk = jnp.tile(g_k, (1, N_KV)).reshape(depth, 1, KVD)
    slope_rows = _prompt_rows(slopes_l2)

    def rows_first(xT):
        return xT.reshape(xT.shape[0], N_KV, HD, xT.shape[-1]).transpose(0, 3, 1, 2)

    def shifted_window(cache, new, wc):
        newT = new.reshape(nseq, n_new, KVD).transpose(0, 2, 1)
        winT = jnp.concatenate([_rows_major_view(cache), newT], axis=2)
        return rows_first(winT[:, :, winT.shape[2] - wc:])

    xp = x_prompt.reshape(n * t, D_MODEL)
    xs = x_sample.reshape(nseq * n_new, D_MODEL)
    swa_kp, swa_vp, swa_ks, swa_vs = [], [], [], []
    moba_kp, moba_vp, moba_ks, moba_vs = [], [], [], []
    for i in range(depth):
        is_swa = i % 2 == 0
        j = i // 2
        qT, kT, vT, katt, vT4, means = _qkv_prompt_call(xp, ga, wqkv_b, gq, gk, i, n, t)
        qs, ks, vs = _qkv_sample_call(xs, ga, wqkv_b, gq, gk, i)
        ks5 = ks.reshape(nseq, n_new, N_KV, HD)
        vs5 = vs.reshape(nseq, n_new, N_KV, HD)
        if is_swa:
            sink_l2 = sinks[j].reshape(N_KV, GROUP).astype(F32) * LOG2E
            attn_p = _swa_prompt_call(qT, katt, vT4, slope_rows, _prompt_rows(sink_l2))
            attn_s = _swa_sample_attn(qs, ks, vs, cache_swa_k[j], cache_swa_v[j],
                                      slopes_l2, sink_l2)
            w = min(WINDOW, t)
            wc = cache_swa_k.shape[2]
            swa_kp.append(rows_first(kT[:, :, t - w:]))
            swa_vp.append(rows_first(vT[:, :, t - w:]))
            swa_ks.append(shifted_window(cache_swa_k[j], ks, wc))
            swa_vs.append(shifted_window(cache_swa_v[j], vs, wc))
        else:
            attn_p = _moba_prompt_call(qT, katt, vT4, means.reshape(n, nb, KVD), slope_rows)
            attn_s = _moba_sample_attn(qs, ks, vs, poolT_k, poolT_v, j, page_table, slopes_l2)
            moba_kp.append(rows_first(kT))
            moba_vp.append(rows_first(vT))
            moba_ks.append(ks5)
            moba_vs.append(vs5)
        xp = _mlp_call(xp, attn_p.reshape(n * t, QD), wo_b, gm, wup_b, wdn_b, i)
        xs = _mlp_call(xs, attn_s, wo_b, gm, wup_b, wdn_b, i)
    return (xp.reshape(n, t, D_MODEL), xs.reshape(nseq, n_new, D_MODEL),
            jnp.stack(swa_kp), jnp.stack(swa_vp), jnp.stack(swa_ks), jnp.stack(swa_vs),
            jnp.stack(moba_kp), jnp.stack(moba_vp), jnp.stack(moba_ks), jnp.stack(moba_vs))
```
